```python
import math
import jax, jax.numpy as jnp
from jax import lax
import numpy as np

D_MODEL = 1024
BATCH = 4
SEQ = 4096
DEPTH = 1
DEC_BATCH = 16
DEC_SEQ = 4096
PAST_LEN = 128

GLA_HEADS = 4
GLA_DK = 64
GLA_DV = 128
GLA_QK = GLA_HEADS * GLA_DK
GLA_V = GLA_HEADS * GLA_DV
GLA_GATE_RANK = 16
GLA_GATE_NORM = 16.0
GLA_CHUNK = 64
SSM_HEAD_DIM = 64
SSM_INNER = 1024
SSM_HEADS = SSM_INNER // SSM_HEAD_DIM
SSM_GROUPS = 4
SSM_HPG = SSM_HEADS // SSM_GROUPS
SSM_STATE = 64
SSM_CONV = 5
SSM_CHUNK = 128
SSM_XBC = SSM_INNER + 2 * SSM_GROUPS * SSM_STATE
D_FF = ((8 * D_MODEL // 3 + 255) // 256) * 256
EPS = 1e-6
IN_SIZES = (GLA_QK, GLA_QK, GLA_V, GLA_V, GLA_GATE_RANK, GLA_GATE_RANK,
            SSM_INNER, SSM_XBC, SSM_HEADS, SSM_HEADS, D_MODEL, D_MODEL)
IN_COLS = sum(IN_SIZES)

kernel_name = "bidir_gla_ssd_gated_hybrid"


def rmsnorm(x, w):
    xf = x.astype(jnp.float32)
    xf = xf * lax.rsqrt(jnp.mean(xf * xf, axis=-1, keepdims=True) + EPS)
    return (xf * w.astype(jnp.float32)).astype(x.dtype)


def split_cols(t, sizes):
    idx = np.cumsum(np.array(sizes))[:-1].tolist()
    return jnp.split(t, idx, axis=-1)


def bidir(fn, fwd_args, bwd_args):
    rev = lambda t: jnp.flip(t, axis=1)
    return fn(*fwd_args, True) + rev(fn(*[rev(a) for a in bwd_args], False))


def gla_scan(q, k, v, log_g, inclusive):
    b, L, h, dk = q.shape
    dv = v.shape[-1]
    Q = GLA_CHUNK
    c = L // Q
    q = q.reshape(b, c, Q, h, dk)
    k = k.reshape(b, c, Q, h, dk)
    v = v.reshape(b, c, Q, h, dv)
    g_cs = jnp.cumsum(log_g.reshape(b, c, Q, h, dk), axis=2)
    q_dec = q * jnp.exp(g_cs)
    att = jnp.einsum('bclhd,bcshd->bchls', q_dec, k * jnp.exp(-g_cs))
    mask = jnp.tril(jnp.ones((Q, Q), dtype=bool), 0 if inclusive else -1)
    att = jnp.where(mask, att, 0.0)
    o = jnp.einsum('bchls,bcshv->bclhv', att, v)
    g_end = g_cs[:, :, -1]
    states = jnp.einsum('bclhd,bclhv->bchdv', k * jnp.exp(g_end[:, :, None] - g_cs), v)

    def step(s, inp):
        dec, st = inp
        return dec[..., None] * s + st, s

    _, s_in = lax.scan(step, jnp.zeros((b, h, dk, dv), q.dtype),
                       (jnp.moveaxis(jnp.exp(g_end), 1, 0), jnp.moveaxis(states, 1, 0)))
    o = o + jnp.einsum('bclhd,cbhdv->bclhv', q_dec, s_in)
    return o.reshape(b, L, h, dv)


def ssd_scan(xh, log_a, bm, cm, inclusive):
    b, L, g, j, p = xh.shape
    n = bm.shape[-1]
    Q = SSM_CHUNK
    c = L // Q
    xh = xh.reshape(b, c, Q, g, j, p)
    bm = bm.reshape(b, c, Q, g, n)
    cm = cm.reshape(b, c, Q, g, n)
    a_cs = jnp.cumsum(log_a.reshape(b, c, Q, g, j), axis=2)
    a_t = jnp.moveaxis(a_cs, 2, -1)
    mask = jnp.tril(jnp.ones((Q, Q), dtype=bool), 0 if inclusive else -1)
    decay = jnp.exp(jnp.where(mask, a_t[..., :, None] - a_t[..., None, :], -jnp.inf))
    cb = jnp.einsum('bclgn,bcsgn->bcgls', cm, bm)
    y = jnp.einsum('bcgls,bcgjls,bcsgjp->bclgjp', cb, decay, xh)
    a_end = a_cs[:, :, -1]
    states = jnp.einsum('bclgn,bclgj,bclgjp->bcgjpn', bm, jnp.exp(a_end[:, :, None] - a_cs), xh)

    def step(s, inp):
        dec, st = inp
        return dec[..., None, None] * s + st, s

    _, s_in = lax.scan(step, jnp.zeros((b, g, j, p, n), xh.dtype),
                       (jnp.moveaxis(jnp.exp(a_end), 1, 0), jnp.moveaxis(states, 1, 0)))
    y = y + jnp.einsum('bclgn,cbgjpn,bclgj->bclgjp', cm, s_in, jnp.exp(a_cs))
    return y.reshape(b, L, g, j, p)


def mixer(u, w_in, gla_up_f, gla_bias_f, gla_up_b, gla_bias_b, gla_norm_w, conv_w, conv_b,
          dt_bias_f, dt_bias_b, a_log_f, a_log_b, d_skip, ssm_norm_w, w_br_gla, w_br_ssm, w_out):
    b, L, _ = u.shape
    proj = (u @ w_in).astype(jnp.float32)
    q, k, v, og, rf, rb, z, xbc, dtf, dtb, gate_gla, gate_ssm = split_cols(proj, IN_SIZES)

    q = q.reshape(b, L, GLA_HEADS, GLA_DK) * (GLA_DK ** -0.5)
    k = k.reshape(b, L, GLA_HEADS, GLA_DK)
    v = v.reshape(b, L, GLA_HEADS, GLA_DV)
    lg_f = (jax.nn.log_sigmoid(rf @ gla_up_f + gla_bias_f) / GLA_GATE_NORM).reshape(b, L, GLA_HEADS, GLA_DK)
    lg_b = (jax.nn.log_sigmoid(rb @ gla_up_b + gla_bias_b) / GLA_GATE_NORM).reshape(b, L, GLA_HEADS, GLA_DK)
    lg_f = lg_f.astype(jnp.float32)
    lg_b = lg_b.astype(jnp.float32)
    o_gla = bidir(gla_scan, (q, k, v, lg_f), (q, k, v, lg_b))
    o_gla = rmsnorm(o_gla, gla_norm_w).reshape(b, L, GLA_V) * jax.nn.silu(og)

    xbc = lax.conv_general_dilated(
        xbc, conv_w[:, None, :].astype(jnp.float32),
        window_strides=(1,),
        padding=((SSM_CONV // 2, SSM_CONV // 2),),
        dimension_numbers=('NWC', 'WIO', 'NWC'),
        feature_group_count=SSM_XBC)
    xbc = jax.nn.silu(xbc + conv_b)
    xs, bm, cm = split_cols(xbc, (SSM_INNER, SSM_GROUPS * SSM_STATE, SSM_GROUPS * SSM_STATE))
    xs = xs.reshape(b, L, SSM_GROUPS, SSM_HPG, SSM_HEAD_DIM)
    bm = bm.reshape(b, L, SSM_GROUPS, SSM_STATE)
    cm = cm.reshape(b, L, SSM_GROUPS, SSM_STATE)

    def dir_inputs(dt_raw, dt_bias, a_log):
        dt = jax.nn.softplus(dt_raw + dt_bias).astype(jnp.float32).reshape(b, L, SSM_GROUPS, SSM_HPG)
        a = -jnp.exp(a_log.astype(jnp.float32)).reshape(SSM_GROUPS, SSM_HPG)
        return (xs * dt[..., None], dt * a, bm, cm)

    y = bidir(ssd_scan, dir_inputs(dtf, dt_bias_f, a_log_f), dir_inputs(dtb, dt_bias_b, a_log_b))
    y = y + d_skip.reshape(SSM_GROUPS, SSM_HPG)[:, :, None] * xs
    y = rmsnorm(y.reshape(b, L, SSM_INNER) * jax.nn.silu(z), ssm_norm_w)

    m = jax.nn.sigmoid(gate_gla) * (o_gla @ w_br_gla) + jax.nn.sigmoid(gate_ssm) * (y @ w_br_ssm)
    return (m @ w_out).astype(u.dtype)


def swiglu(u, w_gate, w_up, w_down):
    return (jax.nn.silu(u @ w_gate) * (u @ w_up)) @ w_down


def setup_inputs(seed: int = 0) -> dict:
    key = jax.random.key(seed)
    ks = iter(jax.random.split(key, 32))
    f32 = jnp.float32

    def nrm(shape, scale):
        return jax.random.normal(next(ks), shape, f32) * scale

    def gain(shape):
        return 1.0 + 0.02 * jax.random.normal(next(ks), shape, f32)

    def dt_bias(shape):
        dt = jnp.exp(jax.random.uniform(next(ks), shape, f32, math.log(1e-3), math.log(1e-1)))
        return dt + jnp.log(-jnp.expm1(-dt))

    def a_log(shape):
        return jnp.log(jax.random.uniform(next(ks), shape, f32, 1.0, 16.0))

    Dp = DEPTH
    return {
        "x_prompt": jax.random.normal(next(ks), (BATCH, SEQ, D_MODEL), f32),
        "x_sample": jax.random.normal(next(ks), (DEC_BATCH, DEC_SEQ, D_MODEL), f32),
        "norm_mix_w": gain((Dp, D_MODEL)),
        "w_in": nrm((Dp, D_MODEL, IN_COLS), D_MODEL ** -0.5),
        "gla_up_f": nrm((Dp, GLA_GATE_RANK, GLA_QK), GLA_GATE_RANK ** -0.5),
        "gla_bias_f": nrm((Dp, GLA_QK), 0.1),
        "gla_up_b": nrm((Dp, GLA_GATE_RANK, GLA_QK), GLA_GATE_RANK ** -0.5),
        "gla_bias_b": nrm((Dp, GLA_QK), 0.1),
        "gla_norm_w": gain((Dp, GLA_DV)),
        "conv_w": nrm((Dp, SSM_CONV, SSM_XBC), SSM_CONV ** -0.5),
        "conv_b": nrm((Dp, SSM_XBC), 0.02),
        "dt_bias_f": dt_bias((Dp, SSM_HEADS)),
        "dt_bias_b": dt_bias((Dp, SSM_HEADS)),
        "a_log_f": a_log((Dp, SSM_HEADS)),
        "a_log_b": a_log((Dp, SSM_HEADS)),
        "d_skip": gain((Dp, SSM_HEADS)),
        "ssm_norm_w": gain((Dp, SSM_INNER)),
        "w_br_gla": nrm((Dp, GLA_V, D_MODEL), GLA_V ** -0.5),
        "w_br_ssm": nrm((Dp, SSM_INNER, D_MODEL), SSM_INNER ** -0.5),
        "w_out": nrm((Dp, D_MODEL, D_MODEL), D_MODEL ** -0.5),
        "norm_ffn_w": gain((Dp, D_MODEL)),
        "w_ffn_gate": nrm((Dp, D_MODEL, D_FF), D_MODEL ** -0.5),
        "w_ffn_up": nrm((Dp, D_MODEL, D_FF), D_MODEL ** -0.5),
        "w_ffn_down": nrm((Dp, D_FF, D_MODEL), D_FF ** -0.5),
        "norm_final_w": gain((D_MODEL,)),
    }


def reference(x_prompt, x_sample, norm_mix_w, w_in, gla_up_f, gla_bias_f, gla_up_b, gla_bias_b,
              gla_norm_w, conv_w, conv_b, dt_bias_f, dt_bias_b, a_log_f, a_log_b, d_skip,
              ssm_norm_w, w_br_gla, w_br_ssm, w_out, norm_ffn_w, w_ffn_gate, w_ffn_up,
              w_ffn_down, norm_final_w):
    def run(x):
        for l in range(DEPTH):
            u = rmsnorm(x, norm_mix_w[l])
            x = x + mixer(u, w_in[l], gla_up_f[l], gla_bias_f[l], gla_up_b[l], gla_bias_b[l],
                          gla_norm_w[l], conv_w[l], conv_b[l], dt_bias_f[l], dt_bias_b[l],
                          a_log_f[l], a_log_b[l], d_skip[l], ssm_norm_w[l], w_br_gla[l],
                          w_br_ssm[l], w_out[l])
            x = x + swiglu(rmsnorm(x, norm_ffn_w[l]), w_ffn_gate[l], w_ffn_up[l], w_ffn_down[l]).astype(x.dtype)
        return rmsnorm(x, norm_final_w)

    y_prompt = run(x_prompt)
    y_sample = run(x_sample)
    return (y_prompt, y_sample)
```

```python
import functools

import jax
import jax.numpy as jnp
from jax import lax
from jax.experimental import pallas as pl
from jax.experimental.pallas import tpu as pltpu

F32 = jnp.float32
BF16 = jnp.bfloat16

D_MODEL = 1024
GLA_HEADS = 4
GLA_DK = 64
GLA_DV = 128
GLA_QK = GLA_HEADS * GLA_DK
GLA_V = GLA_HEADS * GLA_DV
GLA_GATE_RANK = 16
GLA_GATE_NORM = 16.0
SSM_HEAD_DIM = 64
SSM_INNER = 1024
SSM_HEADS = 16
SSM_GROUPS = 4
SSM_HPG = 4
SSM_STATE = 64
SSM_BC = SSM_GROUPS * SSM_STATE
SSM_CONV = 5
SSM_XBC = SSM_INNER + 2 * SSM_BC
D_FF = 2816
EPS = 1e-6
IN_SIZES = (GLA_QK, GLA_QK, GLA_V, GLA_V, GLA_GATE_RANK, GLA_GATE_RANK,
            SSM_INNER, SSM_XBC, SSM_HEADS, SSM_HEADS, D_MODEL, D_MODEL)

SMALL_W = 128
LANE_RF = 0
LANE_RB = 16
LANE_DTF = 32
LANE_DTB = 48

CHUNK = 128
HALO = 16
VMEM_LIMIT = 56 * 1024 * 1024


def _dot(a, b):
    return jnp.dot(a, b, preferred_element_type=F32)


def _dot_nt(a, b):
    return lax.dot_general(a, b, (((1,), (1,)), ((), ())), preferred_element_type=F32)


def _dot_tn(a, b):
    return lax.dot_general(a, b, (((0,), (0,)), ((), ())), preferred_element_type=F32)


def _split3(x):
    hi = x.astype(BF16)
    r = x - hi.astype(F32)
    mid = r.astype(BF16)
    lo = (r - mid.astype(F32)).astype(BF16)
    return hi, mid, lo


def _dot_exact_lhs(m, x):
    hi, mid, lo = _split3(x)
    return _dot(m, hi) + _dot(m, mid) + _dot(m, lo)


def _dot_exact_rhs(x, m):
    hi, mid, lo = _split3(x)
    return _dot(hi, m) + _dot(mid, m) + _dot(lo, m)


def _dot_f32(a, b):
    a_hi = a.astype(BF16)
    a_lo = (a - a_hi.astype(F32)).astype(BF16)
    b_hi = b.astype(BF16)
    b_lo = (b - b_hi.astype(F32)).astype(BF16)
    return _dot(a_hi, b_hi) + _dot(a_hi, b_lo) + _dot(a_lo, b_hi)


def _rms(x, w):
    return x * lax.rsqrt(jnp.mean(x * x, axis=-1, keepdims=True) + EPS) * w


def _sigmoid(x):
    return 1.0 / (1.0 + jnp.exp(-x))


def _silu(x):
    return x * _sigmoid(x)


def _softplus(x):
    return jnp.maximum(x, 0.0) + jnp.log1p(jnp.exp(-jnp.abs(x)))


def _logsig(x):
    return jnp.minimum(x, 0.0) - jnp.log1p(jnp.exp(-jnp.abs(x)))


def _tri_masks(n):
    row = lax.broadcasted_iota(jnp.int32, (n, n), 0)
    col = lax.broadcasted_iota(jnp.int32, (n, n), 1)
    return col <= row, col >= row


def _lane_block(width, block):
    return lax.broadcasted_iota(jnp.int32, (1, width), 1) // block


def _ssd_log_decay_row(alog_row):
    lane = lax.broadcasted_iota(jnp.int32, (1, SMALL_W), 1)
    live = (lane >= LANE_DTF) & (lane < LANE_DTB + SSM_HEADS)
    return jnp.where(live, -jnp.exp(alog_row), 0.0)


def _stack_groups(r):
    return jnp.concatenate(
        [r[g * SSM_STATE:(g + 1) * SSM_STATE, g * 256:(g + 1) * 256] for g in range(SSM_GROUPS)], axis=0)


def _stack_row(row):
    return jnp.concatenate(
        [jnp.broadcast_to(row[:, g * 256:(g + 1) * 256], (SSM_STATE, 256)) for g in range(SSM_GROUPS)], axis=0)


def _inproj_kernel(x_ref, nw_ref, wqk_ref, wv_ref, wog_ref, wz_ref, wxbc_ref, wgg_ref, wgs_ref, wsm_ref,
                   qk_ref, v_ref, og_ref, z_ref, xbc_ref, gg_ref, gs_ref, sm_ref):
    u = _rms(x_ref[...], nw_ref[...]).astype(BF16)
    qk_ref[...] = _dot(u, wqk_ref[...]).astype(BF16)
    v_ref[...] = _dot(u, wv_ref[...]).astype(BF16)
    og_ref[...] = _dot(u, wog_ref[...]).astype(BF16)
    z_ref[...] = _dot(u, wz_ref[...]).astype(BF16)
    xbc_ref[...] = _dot(u, wxbc_ref[...]).astype(BF16)
    gg_ref[...] = _dot(u, wgg_ref[...]).astype(BF16)
    gs_ref[...] = _dot(u, wgs_ref[...]).astype(BF16)
    sm_ref[...] = _dot(u, wsm_ref[...])


def _const_spec(shape):
    nd = len(shape)
    return pl.BlockSpec(shape, lambda *_: (0,) * nd, pipeline_mode=pl.Buffered(1))


def _inproj(x2d, nw, ws, tm):
    n = x2d.shape[0]
    widths = [w.shape[1] for w in ws]
    dtypes = [BF16] * 7 + [F32]
    row = lambda i: (i, 0)
    return pl.pallas_call(
        _inproj_kernel,
        grid=(n // tm,),
        in_specs=[pl.BlockSpec((tm, D_MODEL), row), _const_spec(nw.shape)] + [_const_spec(w.shape) for w in ws],
        out_specs=[pl.BlockSpec((tm, w), row) for w in widths],
        out_shape=[jax.ShapeDtypeStruct((n, w), dt) for w, dt in zip(widths, dtypes)],
        compiler_params=pltpu.CompilerParams(dimension_semantics=("arbitrary",), vmem_limit_bytes=VMEM_LIMIT),
        name="inproj",
    )(x2d, nw, *ws)


def _bwd_state_kernel(k_ref, v_ref, xbc_ref, prev_ref, next_ref, sm_ref,
                      upcb_ref, gbiasb_ref, dtbias_ref, alog_ref, cw_ref, cb_ref, p2b_ref,
                      xa_ref, sgb_ref, ssb_ref,
                      sg_s, ss_s):
    s = pl.program_id(1)
    ns = pl.num_programs(1)
    t = xbc_ref.shape[0]
    nc = t // CHUNK

    @pl.when(s == 0)
    def _():
        sg_s[...] = jnp.zeros_like(sg_s)
        ss_s[...] = jnp.zeros_like(ss_s)

    sgb_ref[...] = sg_s[...]
    ssb_ref[...] = ss_s[...]

    sidx = ns - 1 - s
    has_prev = jnp.where(sidx > 0, 1.0, 0.0)
    has_next = jnp.where(sidx < ns - 1, 1.0, 0.0)
    x = xbc_ref[...].astype(F32)
    prev = prev_ref[...].astype(F32) * has_prev
    nxt = next_ref[...].astype(F32) * has_next
    row8 = lax.broadcasted_iota(jnp.int32, (8, SSM_XBC), 0)

    def shifted(d):
        if d == 0:
            return x
        r = pltpu.roll(x, (-d) % t, 0)
        if d < 0:
            head = r[0:8]
            for i in range(-d):
                head = jnp.where(row8 == i, prev[HALO + d + i:HALO + d + i + 1, :], head)
            return jnp.concatenate([head, r[8:]], axis=0)
        tail = r[t - 8:t]
        for i in range(d):
            tail = jnp.where(row8 == 8 - d + i, nxt[i:i + 1, :], tail)
        return jnp.concatenate([r[:t - 8], tail], axis=0)

    acc = cb_ref[...] + cw_ref[0:1, :] * shifted(-2)
    for i in range(1, SSM_CONV):
        acc = acc + cw_ref[i:i + 1, :] * shifted(i - SSM_CONV // 2)
    xa = _silu(acc).astype(BF16)
    xa_ref[...] = xa

    sm = sm_ref[...]
    lgb = _logsig(_dot_f32(sm, upcb_ref[...]) + gbiasb_ref[...]) * (1.0 / GLA_GATE_NORM)
    dt = _softplus(sm + dtbias_ref[...])
    la = dt * _ssd_log_decay_row(alog_ref[...])
    zz = jnp.concatenate([lgb, la], axis=1)
    ltri, _ = _tri_masks(CHUNK)
    lmat = jnp.where(ltri, 1.0, 0.0).astype(BF16)
    carry = jnp.zeros((1, GLA_QK + SMALL_W), F32)
    parts = []
    for c in range(nc):
        cs = _dot_exact_lhs(lmat, zz[c * CHUNK:(c + 1) * CHUNK]) + carry
        parts.append(cs)
        carry = cs[CHUNK - 1:CHUNK, :]
    ics = jnp.concatenate(parts, axis=0)
    tot = carry
    ecs = ics - zz

    kbs = (k_ref[...].astype(F32) * jnp.exp(ecs[:, :GLA_QK])).astype(BF16)
    head_of_lane = _lane_block(GLA_QK, GLA_DK)
    new_g = sg_s[...] * jnp.exp(tot[:, :GLA_QK])
    for h in range(GLA_HEADS):
        kv = _dot_tn(v_ref[:, h * GLA_DV:(h + 1) * GLA_DV], kbs)
        new_g = new_g + jnp.where(head_of_lane == h, kv, 0.0)
    sg_s[...] = new_g

    p2b = p2b_ref[...]
    wb = jnp.exp(ecs[:, GLA_QK:]) * dt
    wexp = _dot(wb.astype(BF16), p2b)
    xw = (xa[:, :SSM_INNER].astype(F32) * wexp).astype(BF16)
    contrib = _stack_groups(_dot_tn(xa[:, SSM_INNER:SSM_INNER + SSM_BC], xw))
    et = jnp.broadcast_to(jnp.exp(tot[:, GLA_QK:]), (8, SMALL_W))
    dec = _stack_row(_dot_exact_rhs(et, p2b)[0:1, :])
    ss_s[...] = ss_s[...] * dec + contrib


def _bwd_states(qk, v, xbc, sm, upcb, gbiasb, dtbias, alog, cw, cb, p2b, t):
    b, l, _ = qk.shape
    ns = l // t
    hb = t // HALO
    nh = l // HALO
    rev = lambda i, s: (i, ns - 1 - s, 0)
    return pl.pallas_call(
        _bwd_state_kernel,
        grid=(b, ns),
        in_specs=[
            pl.BlockSpec((None, t, GLA_QK), lambda i, s: (i, ns - 1 - s, 1)),
            pl.BlockSpec((None, t, GLA_V), rev),
            pl.BlockSpec((None, t, SSM_XBC), rev),
            pl.BlockSpec((None, HALO, SSM_XBC), lambda i, s: (i, jnp.maximum((ns - 1 - s) * hb - 1, 0), 0)),
            pl.BlockSpec((None, HALO, SSM_XBC), lambda i, s: (i, jnp.minimum((ns - s) * hb, nh - 1), 0)),
            pl.BlockSpec((None, t, SMALL_W), rev),
        ] + [_const_spec(a.shape) for a in (upcb, gbiasb, dtbias, alog, cw, cb, p2b)],
        out_specs=[
            pl.BlockSpec((None, t, SSM_XBC), rev),
            pl.BlockSpec((None, None, GLA_DV, GLA_QK), lambda i, s: (i, ns - 1 - s, 0, 0)),
            pl.BlockSpec((None, None, SSM_BC, 256), lambda i, s: (i, ns - 1 - s, 0, 0)),
        ],
        out_shape=[
            jax.ShapeDtypeStruct((b, l, SSM_XBC), BF16),
            jax.ShapeDtypeStruct((b, ns, GLA_DV, GLA_QK), F32),
            jax.ShapeDtypeStruct((b, ns, SSM_BC, 256), F32),
        ],
        scratch_shapes=[pltpu.VMEM((GLA_DV, GLA_QK), F32), pltpu.VMEM((SSM_BC, 256), F32)],
        compiler_params=pltpu.CompilerParams(dimension_semantics=("arbitrary", "arbitrary"),
                                             vmem_limit_bytes=VMEM_LIMIT),
        name="bwd_states",
    )(qk, v, xbc, xbc, xbc, sm, upcb, gbiasb, dtbias, alog, cw, cb, p2b)


def _mixer_kernel(x_ref, qk_ref, v_ref, og_ref, z_ref, xa_ref, gg_ref, gs_ref, sm_ref, sgb_ref, ssb_ref,
                  upc_ref, gbias_ref, dtbias_ref, alog_ref, glanw_ref, dskip_ref, ssmnw_ref, p2_ref,
                  wbg_ref, wbs_ref, wo_ref,
                  out_ref,
                  sgf_s, ssf_s, lg_s, la_s, dt_s, qf_s, qb_s, emf_s, emb_s, etf_s, etb_s,
                  kvf_s, kvb_s, stf_s, stb_s, dsf_s, dsb_s, decf_s, decb_s, ssf_c, ssb_c, iw_s, o_s):
    s = pl.program_id(1)
    t = x_ref.shape[0]
    nc = t // CHUNK
    C = CHUNK

    @pl.when(s == 0)
    def _():
        sgf_s[...] = jnp.zeros_like(sgf_s)
        ssf_s[...] = jnp.zeros_like(ssf_s)

    sm = sm_ref[...]
    lg_s[...] = _logsig(_dot_f32(sm, upc_ref[...]) + gbias_ref[...]) * (1.0 / GLA_GATE_NORM)
    dt_all = _softplus(sm + dtbias_ref[...])
    dt_s[...] = dt_all
    la_s[...] = dt_all * _ssd_log_decay_row(alog_ref[...])

    tri, triu = _tri_masks(C)
    lmat = jnp.where(tri, 1.0, 0.0).astype(BF16)
    umat = jnp.where(triu, 1.0, 0.0).astype(BF16)
    head_of_lane = _lane_block(GLA_QK, GLA_DK)
    group_of_lane = _lane_block(SSM_BC, SSM_STATE)
    lane128 = lax.broadcasted_iota(jnp.int32, (1, SMALL_W), 1)
    fwd_lane = lane128 < LANE_DTB
    p2 = p2_ref[...]
    zero_bf = jnp.zeros((), BF16)

    def chunk_a(c, carry):
        rows = pl.ds(pl.multiple_of(c * C, C), C)
        la = la_s[rows, :]
        dt = dt_s[rows, :]
        cs = _dot_exact_lhs(lmat, jnp.concatenate([lg_s[rows, 0:GLA_QK], la], axis=1))
        rs = _dot_exact_lhs(umat, jnp.concatenate([lg_s[rows, GLA_QK:2 * GLA_QK], la], axis=1))
        ics, af = cs[:, :GLA_QK], cs[:, GLA_QK:]
        rcs, rb = rs[:, :GLA_QK], rs[:, GLA_QK:]

        m_f = ics[C // 2 - 1:C // 2, :]
        tot_f = ics[C - 1:C, :]
        m_b = rcs[C // 2:C // 2 + 1, :]
        tot_b = rcs[0:1, :]
        q = qk_ref[rows, 0:GLA_QK].astype(F32)
        k = qk_ref[rows, GLA_QK:2 * GLA_QK].astype(F32)
        qf = (q * jnp.exp(ics - m_f)).astype(BF16)
        kf32 = k * jnp.exp(m_f - ics)
        qb = (q * jnp.exp(rcs - m_b)).astype(BF16)
        kb32 = k * jnp.exp(m_b - rcs)
        kf = kf32.astype(BF16)
        kb = kb32.astype(BF16)
        kfs = (kf32 * jnp.exp(tot_f - m_f)).astype(BF16)
        kbs = (kb32 * jnp.exp(tot_b - m_b)).astype(BF16)
        qf_s[rows, :] = qf
        qb_s[rows, :] = qb
        emf_s[c] = jnp.broadcast_to(jnp.exp(m_f), (8, GLA_QK))
        emb_s[c] = jnp.broadcast_to(jnp.exp(m_b), (8, GLA_QK))
        etf_s[c] = jnp.broadcast_to(jnp.exp(tot_f), (8, GLA_QK))
        etb_s[c] = jnp.broadcast_to(jnp.exp(tot_b), (8, GLA_QK))
        kvf = jnp.zeros((GLA_DV, GLA_QK), F32)
        kvb = jnp.zeros((GLA_DV, GLA_QK), F32)
        for h in range(GLA_HEADS):
            hm = head_of_lane == h
            vh = v_ref[rows, h * GLA_DV:(h + 1) * GLA_DV]
            att_f = _dot_nt(jnp.where(hm, qf, zero_bf), kf)
            att_b = _dot_nt(jnp.where(hm, qb, zero_bf), kb)
            att = jnp.where(tri, att_f, att_b).astype(BF16)
            o_s[rows, h * GLA_DV:(h + 1) * GLA_DV] = _dot(att, vh)
            kvf = kvf + jnp.where(hm, _dot_tn(vh, kfs), 0.0)
            kvb = kvb + jnp.where(hm, _dot_tn(vh, kbs), 0.0)
        kvf_s[c] = kvf
        kvb_s[c] = kvb

        xs = xa_ref[rows, 0:SSM_INNER]
        bm = xa_ref[rows, SSM_INNER:SSM_INNER + SSM_BC]
        cm = xa_ref[rows, SSM_INNER + SSM_BC:SSM_XBC]
        tot_a = af[C - 1:C, :]
        tot_r = rb[0:1, :]
        iw = jnp.where(fwd_lane, jnp.exp(af), jnp.exp(rb))
        sw = jnp.where(fwd_lane, jnp.exp(tot_a - af), jnp.exp(tot_r - rb)) * dt
        iw_s[rows, :] = _dot(iw.astype(BF16), p2).astype(BF16)
        swx = _dot(sw.astype(BF16), p2)
        xsf = xs.astype(F32)
        xwf = (xsf * swx[:, :SSM_INNER]).astype(BF16)
        xwb = (xsf * swx[:, SSM_INNER:]).astype(BF16)
        dsf_s[c] = _stack_groups(_dot_tn(bm, xwf))
        dsb_s[c] = _stack_groups(_dot_tn(bm, xwb))
        et = jnp.broadcast_to(jnp.where(fwd_lane, jnp.exp(tot_a), jnp.exp(tot_r)), (8, SMALL_W))
        etx = _dot_exact_rhs(et, p2)
        decf_s[c] = _stack_row(etx[0:1, :SSM_INNER])
        decb_s[c] = _stack_row(etx[0:1, SSM_INNER:])

        af_t = af.T
        rb_t = rb.T
        dt_t = dt.T
        lane_lo = lane128 < SSM_HEAD_DIM
        for g in range(SSM_GROUPS):
            cbm = _dot_nt(jnp.where(group_of_lane == g, cm, zero_bf), bm)
            for pair in range(SSM_HPG // 2):
                blk = (g * SSM_HPG + pair * 2) * SSM_HEAD_DIM
                xpair = xs[:, blk:blk + 2 * SSM_HEAD_DIM]
                ys = []
                for jj in range(2):
                    hd = g * SSM_HPG + pair * 2 + jj
                    lf = LANE_DTF + hd
                    lb = LANE_DTB + hd
                    diff = jnp.where(tri, af[:, lf:lf + 1] - af_t[lf:lf + 1, :],
                                     rb[:, lb:lb + 1] - rb_t[lb:lb + 1, :])
                    sc = jnp.where(tri, dt_t[lf:lf + 1, :], dt_t[lb:lb + 1, :])
                    mm = (cbm * jnp.exp(diff) * sc).astype(BF16)
                    ys.append(_dot(mm, xpair))
                o_s[rows, GLA_V + blk:GLA_V + blk + 2 * SSM_HEAD_DIM] = jnp.where(lane_lo, ys[0], ys[1])
        return carry

    lax.fori_loop(0, nc, chunk_a, 0)

    sg = sgf_s[...]
    ss = ssf_s[...]
    for c in range(nc):
        stf_s[c] = sg
        ssf_c[c] = ss
        sg = sg * etf_s[c][0:1, :] + kvf_s[c]
        ss = ss * decf_s[c] + dsf_s[c]
    sgf_s[...] = sg
    ssf_s[...] = ss
    sg = sgb_ref[...]
    ss = ssb_ref[...]
    for c in reversed(range(nc)):
        stb_s[c] = sg
        ssb_c[c] = ss
        sg = sg * etb_s[c][0:1, :] + kvb_s[c]
        ss = ss * decb_s[c] + dsb_s[c]

    def chunk_c(c, carry):
        rows = pl.ds(pl.multiple_of(c * C, C), C)
        qf = qf_s[rows, :]
        qb = qb_s[rows, :]
        scat = jnp.concatenate([(stf_s[c] * emf_s[c][0:1, :]).astype(BF16),
                                (stb_s[c] * emb_s[c][0:1, :]).astype(BF16)], axis=1)
        for h in range(GLA_HEADS):
            hm = head_of_lane == h
            qcat = jnp.concatenate([jnp.where(hm, qf, zero_bf), jnp.where(hm, qb, zero_bf)], axis=1)
            cols = slice(h * GLA_DV, (h + 1) * GLA_DV)
            o_s[rows, cols] = o_s[rows, cols] + _dot_nt(qcat, scat)
        cm = xa_ref[rows, SSM_INNER + SSM_BC:SSM_XBC]
        scat2 = jnp.concatenate([ssf_c[c].astype(BF16), ssb_c[c].astype(BF16)], axis=1)
        for g in range(SSM_GROUPS):
            zz = _dot(jnp.where(group_of_lane == g, cm, zero_bf), scat2)
            wf = iw_s[rows, g * 256:(g + 1) * 256].astype(F32)
            wb = iw_s[rows, SSM_INNER + g * 256:SSM_INNER + (g + 1) * 256].astype(F32)
            cols = slice(GLA_V + g * 256, GLA_V + (g + 1) * 256)
            o_s[rows, cols] = o_s[rows, cols] + zz[:, :256] * wf + zz[:, 256:] * wb
        return carry

    lax.fori_loop(0, nc, chunk_c, 0)

    og = og_ref[...].astype(F32)
    parts = []
    for h in range(GLA_HEADS):
        oh = o_s[:, h * GLA_DV:(h + 1) * GLA_DV]
        parts.append(oh * lax.rsqrt(jnp.mean(oh * oh, axis=-1, keepdims=True) + EPS))
    gl = (jnp.concatenate(parts, axis=1) * glanw_ref[...] * _silu(og)).astype(BF16)
    xs_all = xa_ref[:, 0:SSM_INNER].astype(F32)
    y = (o_s[:, GLA_V:] + dskip_ref[...] * xs_all) * _silu(z_ref[...].astype(F32))
    y = _rms(y, ssmnw_ref[...]).astype(BF16)
    m = (_sigmoid(gg_ref[...].astype(F32)) * _dot(gl, wbg_ref[...])
         + _sigmoid(gs_ref[...].astype(F32)) * _dot(y, wbs_ref[...]))
    out_ref[...] = x_ref[...] + _dot(m.astype(BF16), wo_ref[...])


def _mixer(x, qk, v, og, z, xa, gg, gs, sm, sgb, ssb, consts, t):
    b, l, _ = x.shape
    ns = l // t
    nc = t // CHUNK
    tok = lambda w: pl.BlockSpec((None, t, w), lambda i, s: (i, s, 0))
    st = lambda r, c: pl.BlockSpec((None, None, r, c), lambda i, s: (i, s, 0, 0))
    scratch = [
        pltpu.VMEM((GLA_DV, GLA_QK), F32),
        pltpu.VMEM((SSM_BC, 256), F32),
        pltpu.VMEM((t, 2 * GLA_QK), F32),
        pltpu.VMEM((t, SMALL_W), F32),
        pltpu.VMEM((t, SMALL_W), F32),
        pltpu.VMEM((t, GLA_QK), BF16),
        pltpu.VMEM((t, GLA_QK), BF16),
        pltpu.VMEM((nc, 8, GLA_QK), F32),
        pltpu.VMEM((nc, 8, GLA_QK), F32),
        pltpu.VMEM((nc, 8, GLA_QK), F32),
        pltpu.VMEM((nc, 8, GLA_QK), F32),
        pltpu.VMEM((nc, GLA_DV, GLA_QK), F32),
        pltpu.VMEM((nc, GLA_DV, GLA_QK), F32),
        pltpu.VMEM((nc, GLA_DV, GLA_QK), F32),
        pltpu.VMEM((nc, GLA_DV, GLA_QK), F32),
        pltpu.VMEM((nc, SSM_BC, 256), F32),
        pltpu.VMEM((nc, SSM_BC, 256), F32),
        pltpu.VMEM((nc, SSM_BC, 256), F32),
        pltpu.VMEM((nc, SSM_BC, 256), F32),
        pltpu.VMEM((nc, SSM_BC, 256), F32),
        pltpu.VMEM((nc, SSM_BC, 256), F32),
        pltpu.VMEM((t, 2 * SSM_INNER), BF16),
        pltpu.VMEM((t, GLA_V + SSM_INNER), F32),
    ]
    return pl.pallas_call(
        _mixer_kernel,
        grid=(b, ns),
        in_specs=[tok(D_MODEL), tok(2 * GLA_QK), tok(GLA_V), tok(GLA_V), tok(SSM_INNER), tok(SSM_XBC),
                  tok(D_MODEL), tok(D_MODEL), tok(SMALL_W), st(GLA_DV, GLA_QK), st(SSM_BC, 256)]
                 + [_const_spec(a.shape) for a in consts],
        out_specs=tok(D_MODEL),
        out_shape=jax.ShapeDtypeStruct((b, l, D_MODEL), F32),
        scratch_shapes=scratch,
        compiler_params=pltpu.CompilerParams(dimension_semantics=("arbitrary", "arbitrary"),
                                             vmem_limit_bytes=VMEM_LIMIT),
        name="mixer",
    )(x, qk, v, og, z, xa, gg, gs, sm, sgb, ssb, *consts)


def _ffn_kernel(x_ref, nw_ref, wg_ref, wu_ref, wd_ref, fw_ref, out_ref):
    x = x_ref[...]
    h = _rms(x, nw_ref[...]).astype(BF16)
    a = (_silu(_dot(h, wg_ref[...])) * _dot(h, wu_ref[...])).astype(BF16)
    y = x + _dot(a, wd_ref[...])
    out_ref[...] = _rms(y, fw_ref[...])


def _ffn(x2d, nw, wg, wu, wd, fw, tm):
    n = x2d.shape[0]
    row = lambda i: (i, 0)
    return pl.pallas_call(
        _ffn_kernel,
        grid=(n // tm,),
        in_specs=[pl.BlockSpec((tm, D_MODEL), row)] + [_const_spec(a.shape) for a in (nw, wg, wu, wd, fw)],
        out_specs=pl.BlockSpec((tm, D_MODEL), row),
        out_shape=jax.ShapeDtypeStruct((n, D_MODEL), F32),
        compiler_params=pltpu.CompilerParams(dimension_semantics=("arbitrary",), vmem_limit_bytes=VMEM_LIMIT),
        name="ffn",
    )(x2d, nw, wg, wu, wd, fw)


def _tile(n, pref):
    t = pref
    while n % t:
        t //= 2
    return t


def _prep_weights(norm_mix_w, w_in, gla_up_f, gla_bias_f, gla_up_b, gla_bias_b, gla_norm_w, conv_w, conv_b,
                  dt_bias_f, dt_bias_b, a_log_f, a_log_b, d_skip, ssm_norm_w, w_br_gla, w_br_ssm, w_out,
                  norm_ffn_w, w_ffn_gate, w_ffn_up, w_ffn_down, norm_final_w):
    offs = [0]
    for sz in IN_SIZES:
        offs.append(offs[-1] + sz)
    col = lambda i: w_in[:, offs[i]:offs[i + 1]]
    wq, wk, wv, wog, wrf, wrb, wz, wxbc, wdtf, wdtb, wgg, wgs = [col(i) for i in range(12)]
    wqk = jnp.concatenate([wq * (GLA_DK ** -0.5), wk], axis=1)
    wsm = jnp.concatenate([wrf, wrb, wdtf, wdtb, jnp.zeros((D_MODEL, SMALL_W - 64), F32)], axis=1)
    proj_ws = [w.astype(BF16) for w in (wqk, wv, wog, wz, wxbc, wgg, wgs, wsm)]

    upc = jnp.zeros((SMALL_W, 2 * GLA_QK), F32)
    upc = upc.at[LANE_RF:LANE_RF + GLA_GATE_RANK, :GLA_QK].set(gla_up_f)
    upc = upc.at[LANE_RB:LANE_RB + GLA_GATE_RANK, GLA_QK:].set(gla_up_b)
    gbias = jnp.concatenate([gla_bias_f, gla_bias_b])[None, :]

    def small_row(f, b):
        r = jnp.zeros((1, SMALL_W), F32)
        r = r.at[0, LANE_DTF:LANE_DTF + SSM_HEADS].set(f)
        return r.at[0, LANE_DTB:LANE_DTB + SSM_HEADS].set(b)

    dtbias = small_row(dt_bias_f, dt_bias_b)
    alog = small_row(a_log_f, a_log_b)

    lane = jnp.arange(SMALL_W)[:, None]
    blk = jnp.arange(SSM_INNER)[None, :] // SSM_HEAD_DIM
    p2f = (lane == LANE_DTF + blk)
    p2b = (lane == LANE_DTB + blk)
    p2 = jnp.concatenate([p2f, p2b], axis=1).astype(BF16)
    p2b = p2b.astype(BF16)

    cw = jnp.concatenate([conv_w, jnp.zeros((8 - SSM_CONV, SSM_XBC), F32)], axis=0)
    return dict(
        nw=norm_mix_w[None, :], proj_ws=proj_ws, upc=upc, gbias=gbias, dtbias=dtbias, alog=alog,
        p2=p2, p2b=p2b, cw=cw, cb=conv_b[None, :],
        glanw=jnp.tile(gla_norm_w, GLA_HEADS)[None, :], dskip=jnp.repeat(d_skip, SSM_HEAD_DIM)[None, :],
        ssmnw=ssm_norm_w[None, :], wbg=w_br_gla.astype(BF16), wbs=w_br_ssm.astype(BF16),
        wo=w_out.astype(BF16), fnw=norm_ffn_w[None, :], wg=w_ffn_gate.astype(BF16),
        wu=w_ffn_up.astype(BF16), wd=w_ffn_down.astype(BF16), finw=norm_final_w[None, :])


def _run(x, p):
    b, l, _ = x.shape
    n = b * l
    t = _tile(l, 512)
    tm = _tile(n, 512)
    qk, v, og, z, xbc, gg, gs, sm = _inproj(x.reshape(n, D_MODEL), p["nw"], p["proj_ws"], tm)
    r3 = lambda a: a.reshape(b, l, a.shape[-1])
    qk, v, og, z, xbc, gg, gs, sm = [r3(a) for a in (qk, v, og, z, xbc, gg, gs, sm)]
    xa, sgb, ssb = _bwd_states(qk, v, xbc, sm, p["upc"][:, GLA_QK:], p["gbias"][:, GLA_QK:], p["dtbias"],
                               p["alog"], p["cw"], p["cb"], p["p2b"], t)
    consts = (p["upc"], p["gbias"], p["dtbias"], p["alog"], p["glanw"], p["dskip"], p["ssmnw"], p["p2"],
              p["wbg"], p["wbs"], p["wo"])
    xm = _mixer(x, qk, v, og, z, xa, gg, gs, sm, sgb, ssb, consts, t)
    y = _ffn(xm.reshape(n, D_MODEL), p["fnw"], p["wg"], p["wu"], p["wd"], p["finw"], tm)
    return y.reshape(b, l, D_MODEL)


def kernel(x_prompt, x_sample, norm_mix_w, w_in, gla_up_f, gla_bias_f, gla_up_b, gla_bias_b, gla_norm_w, conv_w,
           conv_b, dt_bias_f, dt_bias_b, a_log_f, a_log_b, d_skip, ssm_norm_w, w_br_gla, w_br_ssm, w_out,
           norm_ffn_w, w_ffn_gate, w_ffn_up, w_ffn_down, norm_final_w):
    assert norm_mix_w.shape[0] == 1, "single-layer block"
    p = _prep_weights(norm_mix_w[0], w_in[0], gla_up_f[0], gla_bias_f[0], gla_up_b[0], gla_bias_b[0],
                      gla_norm_w[0], conv_w[0], conv_b[0], dt_bias_f[0], dt_bias_b[0], a_log_f[0], a_log_b[0],
                      d_skip[0], ssm_norm_w[0], w_br_gla[0], w_br_ssm[0], w_out[0], norm_ffn_w[0],
                      w_ffn_gate[0], w_ffn_up[0], w_ffn_down[0], norm_final_w)
    return (_run(x_prompt, p), _run(x_sample, p))
```

```python
import functools

import jax
import jax.numpy as jnp
from jax import lax
from jax.experimental import pallas as pl
from jax.experimental.pallas import tpu as pltpu

F32 = jnp.float32
BF16 = jnp.bfloat16

D_MODEL = 1024
GLA_HEADS = 4
GLA_DK = 64
GLA_DV = 128
GLA_QK = GLA_HEADS * GLA_DK
GLA_V = GLA_HEADS * GLA_DV
GLA_GATE_RANK = 16
GLA_GATE_NORM = 16.0
SSM_HEAD_DIM = 64
SSM_INNER = 1024
SSM_HEADS = 16
SSM_GROUPS = 4
SSM_HPG = 4
SSM_STATE = 64
SSM_BC = SSM_GROUPS * SSM_STATE
SSM_CONV = 5
SSM_XBC = SSM_INNER + 2 * SSM_BC
D_FF = 2816
EPS = 1e-6
IN_SIZES = (GLA_QK, GLA_QK, GLA_V, GLA_V, GLA_GATE_RANK, GLA_GATE_RANK,
            SSM_INNER, SSM_XBC, SSM_HEADS, SSM_HEADS, D_MODEL, D_MODEL)

SMALL_W = 128
LANE_RF = 0
LANE_RB = 16
LANE_DTF = 32
LANE_DTB = 48

CHUNK = 128
HALO = 16
VMEM_LIMIT = 56 * 1024 * 1024


def _dot(a, b):
    return jnp.dot(a, b, preferred_element_type=F32)


def _dot_nt(a, b):
    return lax.dot_general(a, b, (((1,), (1,)), ((), ())), preferred_element_type=F32)


def _dot_tn(a, b):
    return lax.dot_general(a, b, (((0,), (0,)), ((), ())), preferred_element_type=F32)


def _split3(x):
    hi = x.astype(BF16)
    r = x - hi.astype(F32)
    mid = r.astype(BF16)
    lo = (r - mid.astype(F32)).astype(BF16)
    return hi, mid, lo


def _split2(x):
    hi = x.astype(BF16)
    return hi, (x - hi.astype(F32)).astype(BF16)


def _dot_exact_lhs(m, x):
    hi, lo = _split2(x)
    return _dot(m, hi) + _dot(m, lo)


def _dot_exact_rhs(x, m):
    hi, mid, lo = _split3(x)
    return _dot(hi, m) + _dot(mid, m) + _dot(lo, m)


def _dot_f32(a, b):
    a_hi = a.astype(BF16)
    a_lo = (a - a_hi.astype(F32)).astype(BF16)
    b_hi = b.astype(BF16)
    b_lo = (b - b_hi.astype(F32)).astype(BF16)
    return _dot(a_hi, b_hi) + _dot(a_hi, b_lo) + _dot(a_lo, b_hi)


def _rms(x, w):
    return x * lax.rsqrt(jnp.mean(x * x, axis=-1, keepdims=True) + EPS) * w


def _sigmoid(x):
    return 0.5 + 0.5 * jnp.tanh(0.5 * x)


def _silu(x):
    h = 0.5 * x
    return h + h * jnp.tanh(h)


def _softplus(x):
    return jnp.maximum(x, 0.0) + jnp.log1p(jnp.exp(-jnp.abs(x)))


def _logsig(x):
    return jnp.minimum(x, 0.0) - jnp.log1p(jnp.exp(-jnp.abs(x)))


def _tri_masks(n):
    row = lax.broadcasted_iota(jnp.int32, (n, n), 0)
    col = lax.broadcasted_iota(jnp.int32, (n, n), 1)
    return col <= row, col >= row


def _lane_block(width, block):
    return lax.broadcasted_iota(jnp.int32, (1, width), 1) // block


def _ssd_log_decay_row(alog_row):
    lane = lax.broadcasted_iota(jnp.int32, (1, SMALL_W), 1)
    live = (lane >= LANE_DTF) & (lane < LANE_DTB + SSM_HEADS)
    return jnp.where(live, -jnp.exp(alog_row), 0.0)


def _stack_groups(r):
    return jnp.concatenate(
        [r[g * SSM_STATE:(g + 1) * SSM_STATE, g * 256:(g + 1) * 256] for g in range(SSM_GROUPS)], axis=0)


def _stack_row(row):
    return jnp.concatenate(
        [jnp.broadcast_to(row[:, g * 256:(g + 1) * 256], (SSM_STATE, 256)) for g in range(SSM_GROUPS)], axis=0)


def _inproj_kernel(x_ref, nw_ref, wqk_ref, wv_ref, wog_ref, wz_ref, wxbc_ref, wgg_ref, wgs_ref, wsm_ref,
                   qk_ref, v_ref, og_ref, z_ref, xbc_ref, gg_ref, gs_ref, sm_ref):
    u = _rms(x_ref[...], nw_ref[...]).astype(BF16)
    qk_ref[...] = _dot(u, wqk_ref[...]).astype(BF16)
    v_ref[...] = _dot(u, wv_ref[...]).astype(BF16)
    og_ref[...] = _dot(u, wog_ref[...]).astype(BF16)
    z_ref[...] = _dot(u, wz_ref[...]).astype(BF16)
    xbc_ref[...] = _dot(u, wxbc_ref[...]).astype(BF16)
    gg_ref[...] = _dot(u, wgg_ref[...]).astype(BF16)
    gs_ref[...] = _dot(u, wgs_ref[...]).astype(BF16)
    sm_ref[...] = _dot(u, wsm_ref[...])


def _const_spec(shape):
    nd = len(shape)
    return pl.BlockSpec(shape, lambda *_: (0,) * nd, pipeline_mode=pl.Buffered(1))


def _inproj(x2d, nw, ws, tm):
    n = x2d.shape[0]
    widths = [w.shape[1] for w in ws]
    dtypes = [BF16] * 7 + [F32]
    row = lambda i: (i, 0)
    return pl.pallas_call(
        _inproj_kernel,
        grid=(n // tm,),
        in_specs=[pl.BlockSpec((tm, D_MODEL), row), _const_spec(nw.shape)] + [_const_spec(w.shape) for w in ws],
        out_specs=[pl.BlockSpec((tm, w), row) for w in widths],
        out_shape=[jax.ShapeDtypeStruct((n, w), dt) for w, dt in zip(widths, dtypes)],
        compiler_params=pltpu.CompilerParams(dimension_semantics=("arbitrary",), vmem_limit_bytes=VMEM_LIMIT),
        name="inproj",
    )(x2d, nw, *ws)


def _bwd_state_kernel(k_ref, v_ref, xbc_ref, prev_ref, next_ref, sm_ref,
                      upcb_ref, gbiasb_ref, dtbias_ref, alog_ref, cw_ref, cb_ref, p2b_ref,
                      xa_ref, sgb_ref, ssb_ref,
                      sg_s, ss_s, xp_s):
    s = pl.program_id(1)
    ns = pl.num_programs(1)
    t = xbc_ref.shape[0]
    nc = t // CHUNK

    @pl.when(s == 0)
    def _():
        sg_s[...] = jnp.zeros_like(sg_s)
        ss_s[...] = jnp.zeros_like(ss_s)

    sgb_ref[...] = sg_s[...]
    ssb_ref[...] = ss_s[...]

    sidx = ns - 1 - s
    has_prev = jnp.where(sidx > 0, 1.0, 0.0)
    has_next = jnp.where(sidx < ns - 1, 1.0, 0.0)
    for j in range(SSM_XBC // 128):
        cols = slice(j * 128, (j + 1) * 128)
        xp_s[j, 8:8 + t, :] = xbc_ref[:, cols].astype(F32)
        xp_s[j, 0:8, :] = prev_ref[HALO - 8:HALO, cols].astype(F32) * has_prev
        xp_s[j, 8 + t:16 + t, :] = next_ref[0:8, cols].astype(F32) * has_next
        acc = cb_ref[:, cols] + cw_ref[0:1, cols] * xp_s[j, 8 - SSM_CONV // 2:8 - SSM_CONV // 2 + t, :]
        for i in range(1, SSM_CONV):
            off = 8 + i - SSM_CONV // 2
            acc = acc + cw_ref[i:i + 1, cols] * xp_s[j, off:off + t, :]
        xa_ref[:, cols] = _silu(acc).astype(BF16)
    xa = xa_ref[...]

    sm = sm_ref[...]
    lgb = _logsig(_dot(sm.astype(BF16), upcb_ref[...]) + gbiasb_ref[...]) * (1.0 / GLA_GATE_NORM)
    dt = _softplus(sm + dtbias_ref[...])
    la = dt * _ssd_log_decay_row(alog_ref[...])
    zz = jnp.concatenate([lgb, la], axis=1)
    ltri, _ = _tri_masks(CHUNK)
    lmat = jnp.where(ltri, 1.0, 0.0).astype(BF16)
    carry = jnp.zeros((1, GLA_QK + SMALL_W), F32)
    parts = []
    for c in range(nc):
        cs = _dot_exact_lhs(lmat, zz[c * CHUNK:(c + 1) * CHUNK]) + carry
        parts.append(cs)
        carry = cs[CHUNK - 1:CHUNK, :]
    ics = jnp.concatenate(parts, axis=0)
    tot = carry
    ecs = ics - zz

    kbs = (k_ref[...].astype(F32) * jnp.exp(ecs[:, :GLA_QK])).astype(BF16)
    head_of_lane = _lane_block(GLA_QK, GLA_DK)
    new_g = sg_s[...] * jnp.exp(tot[:, :GLA_QK])
    for h in range(GLA_HEADS):
        kv = _dot_tn(v_ref[:, h * GLA_DV:(h + 1) * GLA_DV], kbs)
        new_g = new_g + jnp.where(head_of_lane == h, kv, 0.0)
    sg_s[...] = new_g

    p2b = p2b_ref[...]
    wb = jnp.exp(ecs[:, GLA_QK:]) * dt
    wexp = _dot(wb.astype(BF16), p2b)
    xw = (xa[:, :SSM_INNER].astype(F32) * wexp).astype(BF16)
    contrib = jnp.concatenate(
        [_dot_tn(xa[:, SSM_INNER + g * SSM_STATE:SSM_INNER + (g + 1) * SSM_STATE], xw[:, g * 256:(g + 1) * 256])
         for g in range(SSM_GROUPS)], axis=0)
    et = jnp.broadcast_to(jnp.exp(tot[:, GLA_QK:]), (8, SMALL_W))
    dec = _stack_row(_dot_exact_rhs(et, p2b)[0:1, :])
    ss_s[...] = ss_s[...] * dec + contrib


def _bwd_states(qk, v, xbc, sm, upcb, gbiasb, dtbias, alog, cw, cb, p2b, t):
    b, l, _ = qk.shape
    ns = l // t
    hb = t // HALO
    nh = l // HALO
    rev = lambda i, s: (i, ns - 1 - s, 0)
    return pl.pallas_call(
        _bwd_state_kernel,
        grid=(b, ns),
        in_specs=[
            pl.BlockSpec((None, t, GLA_QK), lambda i, s: (i, ns - 1 - s, 1)),
            pl.BlockSpec((None, t, GLA_V), rev),
            pl.BlockSpec((None, t, SSM_XBC), rev),
            pl.BlockSpec((None, HALO, SSM_XBC), lambda i, s: (i, jnp.maximum((ns - 1 - s) * hb - 1, 0), 0)),
            pl.BlockSpec((None, HALO, SSM_XBC), lambda i, s: (i, jnp.minimum((ns - s) * hb, nh - 1), 0)),
            pl.BlockSpec((None, t, SMALL_W), rev),
        ] + [_const_spec(a.shape) for a in (upcb, gbiasb, dtbias, alog, cw, cb, p2b)],
        out_specs=[
            pl.BlockSpec((None, t, SSM_XBC), rev),
            pl.BlockSpec((None, None, GLA_DV, GLA_QK), lambda i, s: (i, ns - 1 - s, 0, 0)),
            pl.BlockSpec((None, None, SSM_BC, 256), lambda i, s: (i, ns - 1 - s, 0, 0)),
        ],
        out_shape=[
            jax.ShapeDtypeStruct((b, l, SSM_XBC), BF16),
            jax.ShapeDtypeStruct((b, ns, GLA_DV, GLA_QK), F32),
            jax.ShapeDtypeStruct((b, ns, SSM_BC, 256), F32),
        ],
        scratch_shapes=[pltpu.VMEM((GLA_DV, GLA_QK), F32), pltpu.VMEM((SSM_BC, 256), F32),
                        pltpu.VMEM((SSM_XBC // 128, t + 16, 128), F32)],
        compiler_params=pltpu.CompilerParams(dimension_semantics=("arbitrary", "arbitrary"),
                                             vmem_limit_bytes=VMEM_LIMIT),
        name="bwd_states",
    )(qk, v, xbc, xbc, xbc, sm, upcb, gbiasb, dtbias, alog, cw, cb, p2b)


def _mixer_kernel(x_ref, qk_ref, v_ref, og_ref, z_ref, xa_ref, gg_ref, gs_ref, sm_ref, sgb_ref, ssb_ref,
                  upc_ref, gbias_ref, dtbias_ref, alog_ref, glanw_ref, dskip_ref, ssmnw_ref, p2_ref,
                  wbg_ref, wbs_ref, wo_ref,
                  out_ref,
                  sgf_s, ssf_s, lg_s, la_s, dt_s, af_s, rb_s, qf_s, qb_s, kf_s, kb_s, qfi_s, qbi_s, etf_s, etb_s,
                  kvf_s, kvb_s, stf_s, stb_s, dsf_s, dsb_s, decf_s, decb_s, ssf_c, ssb_c, o_s):
    s = pl.program_id(1)
    t = x_ref.shape[0]
    nc = t // CHUNK
    C = CHUNK

    @pl.when(s == 0)
    def _():
        sgf_s[...] = jnp.zeros_like(sgf_s)
        ssf_s[...] = jnp.zeros_like(ssf_s)

    sm = sm_ref[...]
    lg_s[...] = _logsig(_dot(sm.astype(BF16), upc_ref[...]) + gbias_ref[...]) * (1.0 / GLA_GATE_NORM)
    dt_all = _softplus(sm + dtbias_ref[...])
    dt_s[...] = dt_all
    la_s[...] = dt_all * _ssd_log_decay_row(alog_ref[...])

    tri, _ = _tri_masks(C)
    lmat = jnp.where(tri, 1.0, 0.0).astype(BF16)
    head_of_lane = _lane_block(GLA_QK, GLA_DK)
    group_of_lane = _lane_block(SSM_BC, SSM_STATE)
    lane128 = lax.broadcasted_iota(jnp.int32, (1, SMALL_W), 1)
    fwd_lane = lane128 < LANE_DTB
    lane_lo = lane128 < SSM_HEAD_DIM
    p2 = p2_ref[...]
    zero_bf = jnp.zeros((), BF16)

    def chunk_a(c, carry):
        rows = slice(c * C, (c + 1) * C)
        la = la_s[rows, :]
        dt = dt_s[rows, :]
        lg = lg_s[rows, :]
        cs = _dot_exact_lhs(lmat, jnp.concatenate([lg, la], axis=1))
        ics, ics_b, af = cs[:, :GLA_QK], cs[:, GLA_QK:2 * GLA_QK], cs[:, 2 * GLA_QK:]
        rcs = ics_b[C - 1:C, :] - ics_b + lg[:, GLA_QK:]
        rb = af[C - 1:C, :] - af + la
        af_s[rows, :] = af
        rb_s[rows, :] = rb

        m_f = ics[C // 2 - 1:C // 2, :]
        tot_f = ics[C - 1:C, :]
        m_b = rcs[C // 2:C // 2 + 1, :]
        tot_b = rcs[0:1, :]
        q = qk_ref[rows, 0:GLA_QK].astype(F32)
        k = qk_ref[rows, GLA_QK:2 * GLA_QK].astype(F32)
        qf32 = q * jnp.exp(ics - m_f)
        kf32 = k * jnp.exp(m_f - ics)
        qb32 = q * jnp.exp(rcs - m_b)
        kb32 = k * jnp.exp(m_b - rcs)
        kfs = (kf32 * jnp.exp(tot_f - m_f)).astype(BF16)
        kbs = (kb32 * jnp.exp(tot_b - m_b)).astype(BF16)
        qf_s[rows, :] = qf32.astype(BF16)
        qb_s[rows, :] = qb32.astype(BF16)
        kf_s[rows, :] = kf32.astype(BF16)
        kb_s[rows, :] = kb32.astype(BF16)
        qfi_s[rows, :] = (qf32 * jnp.exp(m_f)).astype(BF16)
        qbi_s[rows, :] = (qb32 * jnp.exp(m_b)).astype(BF16)
        etf_s[c] = jnp.broadcast_to(jnp.exp(tot_f), (8, GLA_QK))
        etb_s[c] = jnp.broadcast_to(jnp.exp(tot_b), (8, GLA_QK))
        kvf = jnp.zeros((GLA_DV, GLA_QK), F32)
        kvb = jnp.zeros((GLA_DV, GLA_QK), F32)
        for h in range(GLA_HEADS):
            hm = head_of_lane == h
            vh = v_ref[rows, h * GLA_DV:(h + 1) * GLA_DV]
            kvf = kvf + jnp.where(hm, _dot_tn(vh, kfs), 0.0)
            kvb = kvb + jnp.where(hm, _dot_tn(vh, kbs), 0.0)
        kvf_s[c] = kvf
        kvb_s[c] = kvb

        xs = xa_ref[rows, 0:SSM_INNER]
        bm = xa_ref[rows, SSM_INNER:SSM_INNER + SSM_BC]
        tot_a = af[C - 1:C, :]
        tot_r = rb[0:1, :]
        sw = jnp.where(fwd_lane, jnp.exp(tot_a - af), jnp.exp(tot_r - rb)) * dt
        swx = _dot(sw.astype(BF16), p2)
        xsf = xs.astype(F32)
        xwf = (xsf * swx[:, :SSM_INNER]).astype(BF16)
        xwb = (xsf * swx[:, SSM_INNER:]).astype(BF16)
        bms = [bm[:, g * SSM_STATE:(g + 1) * SSM_STATE] for g in range(SSM_GROUPS)]
        dsf_s[c] = jnp.concatenate(
            [_dot_tn(bms[g], xwf[:, g * 256:(g + 1) * 256]) for g in range(SSM_GROUPS)], axis=0)
        dsb_s[c] = jnp.concatenate(
            [_dot_tn(bms[g], xwb[:, g * 256:(g + 1) * 256]) for g in range(SSM_GROUPS)], axis=0)
        et = jnp.broadcast_to(jnp.where(fwd_lane, jnp.exp(tot_a), jnp.exp(tot_r)), (8, SMALL_W))
        etx = _dot_exact_rhs(et, p2)
        decf_s[c] = _stack_row(etx[0:1, :SSM_INNER])
        decb_s[c] = _stack_row(etx[0:1, SSM_INNER:])
        return carry

    for c in range(nc):
        chunk_a(c, 0)

    sg = sgf_s[...]
    ss = ssf_s[...]
    for c in range(nc):
        stf_s[c] = sg.astype(BF16)
        ssf_c[c] = ss.astype(BF16)
        sg = sg * etf_s[c][0:1, :] + kvf_s[c]
        ss = ss * decf_s[c] + dsf_s[c]
    sgf_s[...] = sg
    ssf_s[...] = ss
    sg = sgb_ref[...]
    ss = ssb_ref[...]
    for c in reversed(range(nc)):
        stb_s[c] = sg.astype(BF16)
        ssb_c[c] = ss.astype(BF16)
        sg = sg * etb_s[c][0:1, :] + kvb_s[c]
        ss = ss * decb_s[c] + dsb_s[c]

    tri4 = jnp.concatenate([tri] * 4, axis=1)

    def chunk_o(c, carry):
        rows = slice(c * C, (c + 1) * C)
        qf = qf_s[rows, :]
        qb = qb_s[rows, :]
        kf = kf_s[rows, :]
        kb = kb_s[rows, :]
        kf_m = jnp.concatenate([jnp.where(head_of_lane == h, kf, zero_bf) for h in range(GLA_HEADS)], axis=0)
        kb_m = jnp.concatenate([jnp.where(head_of_lane == h, kb, zero_bf) for h in range(GLA_HEADS)], axis=0)
        att = jnp.where(tri4, _dot_nt(qf, kf_m), _dot_nt(qb, kb_m)).astype(BF16)
        stf = stf_s[c]
        stb = stb_s[c]
        for h in range(GLA_HEADS):
            blk = slice((h // 2) * 128, (h // 2 + 1) * 128)
            hm = (lane128 // GLA_DK) == (h % 2)
            qi = jnp.concatenate([jnp.where(hm, qfi_s[rows, blk], zero_bf),
                                  jnp.where(hm, qbi_s[rows, blk], zero_bf)], axis=1)
            st = jnp.concatenate([stf[:, blk], stb[:, blk]], axis=1)
            vh = v_ref[rows, h * GLA_DV:(h + 1) * GLA_DV]
            o_s[rows, h * GLA_DV:(h + 1) * GLA_DV] = _dot(att[:, h * C:(h + 1) * C], vh) + _dot_nt(qi, st)

        xs = xa_ref[rows, 0:SSM_INNER]
        bm = xa_ref[rows, SSM_INNER:SSM_INNER + SSM_BC]
        cm = xa_ref[rows, SSM_INNER + SSM_BC:SSM_XBC]
        af = af_s[rows, :]
        rb = rb_s[rows, :]
        af_t = af.T
        rb_t = rb.T
        dt_t = dt_s[rows, :].T
        bm_m = jnp.concatenate([jnp.where(group_of_lane == g, bm, zero_bf) for g in range(SSM_GROUPS)], axis=0)
        cbm_all = _dot_nt(cm, bm_m)
        cm32 = cm.astype(F32)
        ssf = ssf_c[c]
        ssb = ssb_c[c]
        for g in range(SSM_GROUPS):
            cbm = cbm_all[:, g * C:(g + 1) * C]
            cblk = cm32[:, (g // 2) * 128:(g // 2 + 1) * 128]
            crot = pltpu.roll(cblk, SSM_STATE, 1)
            cm2 = jnp.where(lane_lo, cblk, crot) if g % 2 == 0 else jnp.where(lane_lo, crot, cblk)
            srow = slice(g * SSM_STATE, (g + 1) * SSM_STATE)
            for pair in range(SSM_HPG // 2):
                blk = (g * SSM_HPG + pair * 2) * SSM_HEAD_DIM
                pl_ = slice(pair * 128, (pair + 1) * 128)
                rhs = jnp.concatenate([xs[:, blk:blk + 128], ssf[srow, pl_], ssb[srow, pl_]], axis=0)
                ys = []
                for jj in range(2):
                    hd = g * SSM_HPG + pair * 2 + jj
                    lf = LANE_DTF + hd
                    lb = LANE_DTB + hd
                    a_col = jnp.broadcast_to(af[:, lf:lf + 1], (C, C))
                    r_col = jnp.broadcast_to(rb[:, lb:lb + 1], (C, C))
                    diff = jnp.where(tri, a_col - af_t[lf:lf + 1, :], r_col - rb_t[lb:lb + 1, :])
                    sc = jnp.where(tri, dt_t[lf:lf + 1, :], dt_t[lb:lb + 1, :])
                    mm = (cbm * jnp.exp(diff) * sc).astype(BF16)
                    ecol = jnp.where(lane_lo, jnp.exp(a_col), jnp.exp(r_col))
                    lhs = jnp.concatenate([mm, (cm2 * ecol).astype(BF16)], axis=1)
                    ys.append(_dot(lhs, rhs))
                o_s[rows, GLA_V + blk:GLA_V + blk + 128] = jnp.where(lane_lo, ys[0], ys[1])
        return carry

    for c in range(nc):
        chunk_o(c, 0)

    og = og_ref[...].astype(F32)
    parts = []
    for h in range(GLA_HEADS):
        oh = o_s[:, h * GLA_DV:(h + 1) * GLA_DV]
        parts.append(oh * lax.rsqrt(jnp.mean(oh * oh, axis=-1, keepdims=True) + EPS))
    gl = (jnp.concatenate(parts, axis=1) * glanw_ref[...] * _silu(og)).astype(BF16)
    xs_all = xa_ref[:, 0:SSM_INNER].astype(F32)
    y = (o_s[:, GLA_V:] + dskip_ref[...] * xs_all) * _silu(z_ref[...].astype(F32))
    y = _rms(y, ssmnw_ref[...]).astype(BF16)
    m = (_sigmoid(gg_ref[...].astype(F32)) * _dot(gl, wbg_ref[...])
         + _sigmoid(gs_ref[...].astype(F32)) * _dot(y, wbs_ref[...]))
    out_ref[...] = x_ref[...] + _dot(m.astype(BF16), wo_ref[...])


def _mixer(x, qk, v, og, z, xa, gg, gs, sm, sgb, ssb, consts, t):
    b, l, _ = x.shape
    ns = l // t
    nc = t // CHUNK
    tok = lambda w: pl.BlockSpec((None, t, w), lambda i, s: (i, s, 0))
    st = lambda r, c: pl.BlockSpec((None, None, r, c), lambda i, s: (i, s, 0, 0))
    scratch = [
        pltpu.VMEM((GLA_DV, GLA_QK), F32),
        pltpu.VMEM((SSM_BC, 256), F32),
        pltpu.VMEM((t, 2 * GLA_QK), F32),
        pltpu.VMEM((t, SMALL_W), F32),
        pltpu.VMEM((t, SMALL_W), F32),
        pltpu.VMEM((t, SMALL_W), F32),
        pltpu.VMEM((t, SMALL_W), F32),
        pltpu.VMEM((t, GLA_QK), BF16),
        pltpu.VMEM((t, GLA_QK), BF16),
        pltpu.VMEM((t, GLA_QK), BF16),
        pltpu.VMEM((t, GLA_QK), BF16),
        pltpu.VMEM((t, GLA_QK), BF16),
        pltpu.VMEM((t, GLA_QK), BF16),
        pltpu.VMEM((nc, 8, GLA_QK), F32),
        pltpu.VMEM((nc, 8, GLA_QK), F32),
        pltpu.VMEM((nc, GLA_DV, GLA_QK), F32),
        pltpu.VMEM((nc, GLA_DV, GLA_QK), F32),
        pltpu.VMEM((nc, GLA_DV, GLA_QK), BF16),
        pltpu.VMEM((nc, GLA_DV, GLA_QK), BF16),
        pltpu.VMEM((nc, SSM_BC, 256), F32),
        pltpu.VMEM((nc, SSM_BC, 256), F32),
        pltpu.VMEM((nc, SSM_BC, 256), F32),
        pltpu.VMEM((nc, SSM_BC, 256), F32),
        pltpu.VMEM((nc, SSM_BC, 256), BF16),
        pltpu.VMEM((nc, SSM_BC, 256), BF16),
        pltpu.VMEM((t, GLA_V + SSM_INNER), F32),
    ]
    return pl.pallas_call(
        _mixer_kernel,
        grid=(b, ns),
        in_specs=[tok(D_MODEL), tok(2 * GLA_QK), tok(GLA_V), tok(GLA_V), tok(SSM_INNER), tok(SSM_XBC),
                  tok(D_MODEL), tok(D_MODEL), tok(SMALL_W), st(GLA_DV, GLA_QK), st(SSM_BC, 256)]
                 + [_const_spec(a.shape) for a in consts],
        out_specs=tok(D_MODEL),
        out_shape=jax.ShapeDtypeStruct((b, l, D_MODEL), F32),
        scratch_shapes=scratch,
        compiler_params=pltpu.CompilerParams(dimension_semantics=("arbitrary", "arbitrary"),
                                             vmem_limit_bytes=VMEM_LIMIT),
        name="mixer",
    )(x, qk, v, og, z, xa, gg, gs, sm, sgb, ssb, *consts)


def _ffn_kernel(x_ref, nw_ref, wg_ref, wu_ref, wd_ref, fw_ref, out_ref):
    x = x_ref[...]
    h = _rms(x, nw_ref[...]).astype(BF16)
    a = (_silu(_dot(h, wg_ref[...])) * _dot(h, wu_ref[...])).astype(BF16)
    y = x + _dot(a, wd_ref[...])
    out_ref[...] = _rms(y, fw_ref[...])


def _ffn(x2d, nw, wg, wu, wd, fw, tm):
    n = x2d.shape[0]
    row = lambda i: (i, 0)
    return pl.pallas_call(
        _ffn_kernel,
        grid=(n // tm,),
        in_specs=[pl.BlockSpec((tm, D_MODEL), row)] + [_const_spec(a.shape) for a in (nw, wg, wu, wd, fw)],
        out_specs=pl.BlockSpec((tm, D_MODEL), row),
        out_shape=jax.ShapeDtypeStruct((n, D_MODEL), F32),
        compiler_params=pltpu.CompilerParams(dimension_semantics=("arbitrary",), vmem_limit_bytes=VMEM_LIMIT),
        name="ffn",
    )(x2d, nw, wg, wu, wd, fw)


def _tile(n, pref):
    t = pref
    while n % t:
        t //= 2
    return t


def _prep_weights(norm_mix_w, w_in, gla_up_f, gla_bias_f, gla_up_b, gla_bias_b, gla_norm_w, conv_w, conv_b,
                  dt_bias_f, dt_bias_b, a_log_f, a_log_b, d_skip, ssm_norm_w, w_br_gla, w_br_ssm, w_out,
                  norm_ffn_w, w_ffn_gate, w_ffn_up, w_ffn_down, norm_final_w):
    offs = [0]
    for sz in IN_SIZES:
        offs.append(offs[-1] + sz)
    col = lambda i: w_in[:, offs[i]:offs[i + 1]]
    wq, wk, wv, wog, wrf, wrb, wz, wxbc, wdtf, wdtb, wgg, wgs = [col(i) for i in range(12)]
    wqk = jnp.concatenate([wq * (GLA_DK ** -0.5), wk], axis=1)
    wsm = jnp.concatenate([wrf, wrb, wdtf, wdtb, jnp.zeros((D_MODEL, SMALL_W - 64), F32)], axis=1)
    proj_ws = [w.astype(BF16) for w in (wqk, wv, wog, wz, wxbc, wgg, wgs, wsm)]

    upc = jnp.zeros((SMALL_W, 2 * GLA_QK), F32)
    upc = upc.at[LANE_RF:LANE_RF + GLA_GATE_RANK, :GLA_QK].set(gla_up_f)
    upc = upc.at[LANE_RB:LANE_RB + GLA_GATE_RANK, GLA_QK:].set(gla_up_b)
    upc = upc.astype(BF16)
    gbias = jnp.concatenate([gla_bias_f, gla_bias_b])[None, :]

    def small_row(f, b):
        r = jnp.zeros((1, SMALL_W), F32)
        r = r.at[0, LANE_DTF:LANE_DTF + SSM_HEADS].set(f)
        return r.at[0, LANE_DTB:LANE_DTB + SSM_HEADS].set(b)

    dtbias = small_row(dt_bias_f, dt_bias_b)
    alog = small_row(a_log_f, a_log_b)

    lane = jnp.arange(SMALL_W)[:, None]
    blk = jnp.arange(SSM_INNER)[None, :] // SSM_HEAD_DIM
    p2f = (lane == LANE_DTF + blk)
    p2b = (lane == LANE_DTB + blk)
    p2 = jnp.concatenate([p2f, p2b], axis=1).astype(BF16)
    p2b = p2b.astype(BF16)

    cw = jnp.concatenate([conv_w, jnp.zeros((8 - SSM_CONV, SSM_XBC), F32)], axis=0)
    return dict(
        nw=norm_mix_w[None, :], proj_ws=proj_ws, upc=upc, gbias=gbias, dtbias=dtbias, alog=alog,
        p2=p2, p2b=p2b, cw=cw, cb=conv_b[None, :],
        glanw=jnp.tile(gla_norm_w, GLA_HEADS)[None, :], dskip=jnp.repeat(d_skip, SSM_HEAD_DIM)[None, :],
        ssmnw=ssm_norm_w[None, :], wbg=w_br_gla.astype(BF16), wbs=w_br_ssm.astype(BF16),
        wo=w_out.astype(BF16), fnw=norm_ffn_w[None, :], wg=w_ffn_gate.astype(BF16),
        wu=w_ffn_up.astype(BF16), wd=w_ffn_down.astype(BF16), finw=norm_final_w[None, :])


def _run(x, p):
    b, l, _ = x.shape
    n = b * l
    t = _tile(l, 512)
    tm = _tile(n, 512)
    qk, v, og, z, xbc, gg, gs, sm = _inproj(x.reshape(n, D_MODEL), p["nw"], p["proj_ws"], tm)
    r3 = lambda a: a.reshape(b, l, a.shape[-1])
    qk, v, og, z, xbc, gg, gs, sm = [r3(a) for a in (qk, v, og, z, xbc, gg, gs, sm)]
    xa, sgb, ssb = _bwd_states(qk, v, xbc, sm, p["upc"][:, GLA_QK:], p["gbias"][:, GLA_QK:], p["dtbias"],
                               p["alog"], p["cw"], p["cb"], p["p2b"], t)
    consts = (p["upc"], p["gbias"], p["dtbias"], p["alog"], p["glanw"], p["dskip"], p["ssmnw"], p["p2"],
              p["wbg"], p["wbs"], p["wo"])
    xm = _mixer(x, qk, v, og, z, xa, gg, gs, sm, sgb, ssb, consts, t)
    y = _ffn(xm.reshape(n, D_MODEL), p["fnw"], p["wg"], p["wu"], p["wd"], p["finw"], tm)
    return y.reshape(b, l, D_MODEL)


def kernel(x_prompt, x_sample, norm_mix_w, w_in, gla_up_f, gla_bias_f, gla_up_b, gla_bias_b, gla_norm_w, conv_w,
           conv_b, dt_bias_f, dt_bias_b, a_log_f, a_log_b, d_skip, ssm_norm_w, w_br_gla, w_br_ssm, w_out,
           norm_ffn_w, w_ffn_gate, w_ffn_up, w_ffn_down, norm_final_w):
    assert norm_mix_w.shape[0] == 1, "single-layer block"
    p = _prep_weights(norm_mix_w[0], w_in[0], gla_up_f[0], gla_bias_f[0], gla_up_b[0], gla_bias_b[0],
                      gla_norm_w[0], conv_w[0], conv_b[0], dt_bias_f[0], dt_bias_b[0], a_log_f[0], a_log_b[0],
                      d_skip[0], ssm_norm_w[0], w_br_gla[0], w_br_ssm[0], w_out[0], norm_ffn_w[0],
                      w_ffn_gate[0], w_ffn_up[0], w_ffn_down[0], norm_final_w)
    return (_run(x_prompt, p), _run(x_sample, p))
```

```python
import functools
import math

import jax
import jax.numpy as jnp
from jax import lax
from jax.experimental import pallas as pl
from jax.experimental.pallas import tpu as pltpu

F32 = jnp.float32
BF16 = jnp.bfloat16

D_MODEL = 1024
GLA_HEADS = 4
GLA_DK = 64
GLA_DV = 128
GLA_QK = GLA_HEADS * GLA_DK
GLA_V = GLA_HEADS * GLA_DV
GLA_GATE_RANK = 16
GLA_GATE_NORM = 16.0
SSM_HEAD_DIM = 64
SSM_INNER = 1024
SSM_HEADS = 16
SSM_GROUPS = 4
SSM_HPG = 4
SSM_STATE = 64
SSM_BC = SSM_GROUPS * SSM_STATE
SSM_CONV = 5
SSM_XBC = SSM_INNER + 2 * SSM_BC
D_FF = 2816
EPS = 1e-6
LOG2E = math.log2(math.e)
IN_SIZES = (GLA_QK, GLA_QK, GLA_V, GLA_V, GLA_GATE_RANK, GLA_GATE_RANK,
            SSM_INNER, SSM_XBC, SSM_HEADS, SSM_HEADS, D_MODEL, D_MODEL)

SMALL_W = 128
LANE_RF = 0
LANE_RB = 16
LANE_DTF = 32
LANE_DTB = 48

LANES = 128
CHUNK = 128
HALO = 8
VMEM_LIMIT = 56 * 1024 * 1024


def _dot(a, b):
    return jnp.dot(a, b, preferred_element_type=F32)


def _dot_nt(a, b):
    return lax.dot_general(a, b, (((1,), (1,)), ((), ())), preferred_element_type=F32)


def _dot_tn(a, b):
    return lax.dot_general(a, b, (((0,), (0,)), ((), ())), preferred_element_type=F32)


def _split2(x):
    hi = x.astype(BF16)
    return hi, (x - hi.astype(F32)).astype(BF16)


def _dot_exact_lhs(m, x):
    hi, lo = _split2(x)
    return _dot(m, hi) + _dot(m, lo)


def _dot_exact_rhs(x, m):
    hi, lo = _split2(x)
    return _dot(hi, m) + _dot(lo, m)


def _rms(x, w):
    return x * lax.rsqrt(jnp.mean(x * x, axis=-1, keepdims=True) + EPS) * w


def _sigmoid_of_half(h):
    return 0.5 + 0.5 * jnp.tanh(h)


def _silu_of_half(h):
    return h + h * jnp.tanh(h)


def _silu(x):
    return _silu_of_half(0.5 * x)


def _softplus(x):
    return jnp.maximum(x, 0.0) + jnp.log1p(jnp.exp(-jnp.abs(x)))


def _logsig(x):
    return jnp.minimum(x, 0.0) - jnp.log1p(jnp.exp(-jnp.abs(x)))


def _tri_mask(n):
    row = lax.broadcasted_iota(jnp.int32, (n, n), 0)
    col = lax.broadcasted_iota(jnp.int32, (n, n), 1)
    return col <= row


def _lane_block(width, block):
    return lax.broadcasted_iota(jnp.int32, (1, width), 1) // block


def _stack_row(row):
    return jnp.concatenate(
        [jnp.broadcast_to(row[:, g * 256:(g + 1) * 256], (SSM_STATE, 256)) for g in range(SSM_GROUPS)], axis=0)


def _inproj_kernel(tiles_per_row, x_ref, xp_ref, xn_ref, nw_ref,
                   wqk_ref, wv_ref, wog_ref, wz_ref, wxbc_ref, wgg_ref, wgs_ref, wsm_ref,
                   upc_ref, gbias_ref, dtbias_ref, alog_ref, cwh_ref, cbh_ref,
                   qk_ref, v_ref, sog_ref, sz_ref, xa_ref, sgg_ref, sgs_ref, lg_ref, la_ref, ldt_ref,
                   xp_s):
    i = pl.program_id(0)
    tm = x_ref.shape[0]
    nw = nw_ref[...]
    u = _rms(x_ref[...], nw).astype(BF16)

    pos = i % tiles_per_row
    has_prev = jnp.where(pos > 0, 1.0, 0.0)
    has_next = jnp.where(pos < tiles_per_row - 1, 1.0, 0.0)
    xh = jnp.concatenate([xp_ref[...], xn_ref[...]], axis=0)
    xbc_h = _dot(_rms(xh, nw).astype(BF16), wxbc_ref[...])
    xbc = _dot(u, wxbc_ref[...])
    n_slab = SSM_XBC // LANES
    for j in range(n_slab):
        cols = slice(j * LANES, (j + 1) * LANES)
        xp_s[j, HALO:HALO + tm, :] = xbc[:, cols]
        xp_s[j, 0:HALO, :] = xbc_h[0:HALO, cols] * has_prev
        xp_s[j, HALO + tm:2 * HALO + tm, :] = xbc_h[HALO:2 * HALO, cols] * has_next

    def conv_slab(j):
        cols = slice(j * LANES, (j + 1) * LANES)
        acc = cbh_ref[:, cols] + cwh_ref[0:1, cols] * xp_s[j, HALO - SSM_CONV // 2:HALO - SSM_CONV // 2 + tm, :]
        for tap in range(1, SSM_CONV):
            off = HALO + tap - SSM_CONV // 2
            acc = acc + cwh_ref[tap:tap + 1, cols] * xp_s[j, off:off + tm, :]
        xa_ref[:, cols] = _silu_of_half(acc).astype(BF16)

    sm = _dot(u, wsm_ref[...])
    lg_ref[...] = _logsig(_dot(sm.astype(BF16), upc_ref[...]) + gbias_ref[...]) * (LOG2E / GLA_GATE_NORM)
    dt = _softplus(sm + dtbias_ref[...])
    lane = lax.broadcasted_iota(jnp.int32, (1, SMALL_W), 1)
    live = (lane >= LANE_DTF) & (lane < LANE_DTB + SSM_HEADS)
    la_ref[...] = dt * jnp.where(live, -LOG2E * jnp.exp(alog_ref[...]), 0.0)
    ldt_ref[...] = jnp.log2(dt)

    groups = [
        lambda: qk_ref.__setitem__(..., _dot(u, wqk_ref[...]).astype(BF16)),
        lambda: v_ref.__setitem__(..., _dot(u, wv_ref[...]).astype(BF16)),
        lambda: sog_ref.__setitem__(..., _silu_of_half(_dot(u, wog_ref[...])).astype(BF16)),
        lambda: sz_ref.__setitem__(..., _silu_of_half(_dot(u, wz_ref[...])).astype(BF16)),
        lambda: sgg_ref.__setitem__(..., _sigmoid_of_half(_dot(u, wgg_ref[...])).astype(BF16)),
        lambda: sgs_ref.__setitem__(..., _sigmoid_of_half(_dot(u, wgs_ref[...])).astype(BF16)),
    ]
    per = n_slab // len(groups)
    for gi, emit in enumerate(groups):
        for j in range(gi * per, (gi + 1) * per):
            conv_slab(j)
        emit()


def _const_spec(shape):
    nd = len(shape)
    return pl.BlockSpec(shape, lambda *_: (0,) * nd, pipeline_mode=pl.Buffered(1))


def _inproj(x2d, seq_len, nw, ws, consts, tm):
    n = x2d.shape[0]
    hb = tm // HALO
    nh = n // HALO
    widths = [2 * GLA_QK, GLA_V, GLA_V, SSM_INNER, SSM_XBC, D_MODEL, D_MODEL, 2 * GLA_QK, SMALL_W, SMALL_W]
    dtypes = [BF16] * 7 + [F32] * 3
    row = lambda i: (i, 0)
    return pl.pallas_call(
        functools.partial(_inproj_kernel, seq_len // tm),
        grid=(n // tm,),
        in_specs=[pl.BlockSpec((tm, D_MODEL), row),
                  pl.BlockSpec((HALO, D_MODEL), lambda i: (jnp.maximum(i * hb - 1, 0), 0)),
                  pl.BlockSpec((HALO, D_MODEL), lambda i: (jnp.minimum((i + 1) * hb, nh - 1), 0)),
                  _const_spec(nw.shape)]
                 + [_const_spec(w.shape) for w in ws] + [_const_spec(a.shape) for a in consts],
        out_specs=[pl.BlockSpec((tm, w), row) for w in widths],
        out_shape=[jax.ShapeDtypeStruct((n, w), dt) for w, dt in zip(widths, dtypes)],
        scratch_shapes=[pltpu.VMEM((SSM_XBC // LANES, tm + 2 * HALO, LANES), F32)],
        compiler_params=pltpu.CompilerParams(dimension_semantics=("arbitrary",), vmem_limit_bytes=VMEM_LIMIT),
        name="inproj",
    )(x2d, x2d, x2d, nw, *ws, *consts)


def _bwd_state_kernel(k_ref, v_ref, xa_ref, lgb_ref, la_ref, ldt_ref, p2b_ref,
                      sgb_ref, ssb_ref,
                      sg_s, ss_s):
    s = pl.program_id(1)
    t = k_ref.shape[0]
    nc = t // CHUNK

    @pl.when(s == 0)
    def _():
        sg_s[...] = jnp.zeros_like(sg_s)
        ss_s[...] = jnp.zeros_like(ss_s)

    sgb_ref[...] = sg_s[...]
    ssb_ref[...] = ss_s[...]

    zz = jnp.concatenate([lgb_ref[...], la_ref[...]], axis=1)
    lmat = jnp.where(_tri_mask(CHUNK), 1.0, 0.0).astype(BF16)
    carry = jnp.zeros((1, GLA_QK + SMALL_W), F32)
    parts = []
    for c in range(nc):
        cs = _dot_exact_lhs(lmat, zz[c * CHUNK:(c + 1) * CHUNK]) + carry
        parts.append(cs)
        carry = cs[CHUNK - 1:CHUNK, :]
    ics = jnp.concatenate(parts, axis=0)
    tot = carry
    ecs = ics - zz

    kbs = (k_ref[...].astype(F32) * jnp.exp2(ecs[:, :GLA_QK])).astype(BF16)
    head_of_lane = _lane_block(GLA_QK, GLA_DK)
    new_g = sg_s[...] * jnp.exp2(tot[:, :GLA_QK])
    vt_k = _dot_tn(v_ref[...], kbs)
    for h in range(GLA_HEADS):
        new_g = new_g + jnp.where(head_of_lane == h, vt_k[h * GLA_DV:(h + 1) * GLA_DV, :], 0.0)
    sg_s[...] = new_g

    p2b = p2b_ref[...]
    wb = jnp.exp2(ecs[:, GLA_QK:] + ldt_ref[...])
    wexp = _dot(wb.astype(BF16), p2b)
    xw = (xa_ref[:, 0:SSM_INNER].astype(F32) * wexp).astype(BF16)
    contrib = jnp.concatenate(
        [_dot_tn(xa_ref[:, SSM_INNER + g * SSM_STATE:SSM_INNER + (g + 1) * SSM_STATE], xw[:, g * 256:(g + 1) * 256])
         for g in range(SSM_GROUPS)], axis=0)
    et = jnp.broadcast_to(jnp.exp2(tot[:, GLA_QK:]), (8, SMALL_W))
    dec = _stack_row(_dot_exact_rhs(et, p2b)[0:1, :])
    ss_s[...] = ss_s[...] * dec + contrib


def _bwd_states(qk, v, xa, lg, la, ldt, p2b, t):
    b, l, _ = qk.shape
    ns = l // t
    rev = lambda i, s: (i, ns - 1 - s, 0)
    rev1 = lambda i, s: (i, ns - 1 - s, 1)
    return pl.pallas_call(
        _bwd_state_kernel,
        grid=(b, ns),
        in_specs=[
            pl.BlockSpec((None, t, GLA_QK), rev1),
            pl.BlockSpec((None, t, GLA_V), rev),
            pl.BlockSpec((None, t, SSM_XBC), rev),
            pl.BlockSpec((None, t, GLA_QK), rev1),
            pl.BlockSpec((None, t, SMALL_W), rev),
            pl.BlockSpec((None, t, SMALL_W), rev),
            _const_spec(p2b.shape),
        ],
        out_specs=[
            pl.BlockSpec((None, None, GLA_DV, GLA_QK), lambda i, s: (i, ns - 1 - s, 0, 0)),
            pl.BlockSpec((None, None, SSM_BC, 256), lambda i, s: (i, ns - 1 - s, 0, 0)),
        ],
        out_shape=[
            jax.ShapeDtypeStruct((b, ns, GLA_DV, GLA_QK), F32),
            jax.ShapeDtypeStruct((b, ns, SSM_BC, 256), F32),
        ],
        scratch_shapes=[pltpu.VMEM((GLA_DV, GLA_QK), F32), pltpu.VMEM((SSM_BC, 256), F32)],
        compiler_params=pltpu.CompilerParams(dimension_semantics=("arbitrary", "arbitrary"),
                                             vmem_limit_bytes=VMEM_LIMIT),
        name="bwd_states",
    )(qk, v, xa, lg, la, ldt, p2b)


def _mixer_kernel(x_ref, qk_ref, v_ref, sog_ref, sz_ref, xa_ref, sgg_ref, sgs_ref, lg_ref, la_ref, ldt_ref,
                  sgb_ref, ssb_ref,
                  glanw_ref, dskip_ref, ssmnw_ref, p2_ref, wbg_ref, wbs_ref, wo_ref,
                  out_ref,
                  sgf_s, ssf_s, af_s, rb_s, qf_s, qb_s, kf_s, kb_s, qfi_s, qbi_s, etf_s, etb_s,
                  kvf_s, kvb_s, stf_s, stb_s, dsf_s, dsb_s, decf_s, decb_s, ssf_c, ssb_c, o_s, *, ns):
    g = pl.program_id(0)
    t = x_ref.shape[0]
    nc = t // CHUNK
    C = CHUNK

    @pl.when(g % ns == 0)
    def _():
        sgf_s[...] = jnp.zeros_like(sgf_s)
        ssf_s[...] = jnp.zeros_like(ssf_s)

    @pl.when(g == 0)
    def _():
        o_s[...] = jnp.zeros_like(o_s)

    slot_w = g % 2
    slot_r = 1 - slot_w

    def finish(part, nparts):
        rr = slice(part * (t // nparts), (part + 1) * (t // nparts))
        parts = []
        for h in range(GLA_HEADS):
            oh = o_s[slot_r, rr, h * GLA_DV:(h + 1) * GLA_DV]
            parts.append(oh * lax.rsqrt(jnp.mean(oh * oh, axis=-1, keepdims=True) + EPS))
        gl = (jnp.concatenate(parts, axis=1) * glanw_ref[...] * sog_ref[rr, :].astype(F32)).astype(BF16)
        y = _rms(o_s[slot_r, rr, GLA_V:] * sz_ref[rr, :].astype(F32), ssmnw_ref[...]).astype(BF16)
        m = (sgg_ref[rr, :].astype(F32) * _dot(gl, wbg_ref[...])
             + sgs_ref[rr, :].astype(F32) * _dot(y, wbs_ref[...]))
        out_ref[rr, :] = x_ref[rr, :] + _dot(m.astype(BF16), wo_ref[...])

    tri = _tri_mask(C)
    lmat = jnp.where(tri, 1.0, 0.0).astype(BF16)
    head_of_lane = _lane_block(GLA_QK, GLA_DK)
    group_of_lane = _lane_block(SSM_BC, SSM_STATE)
    lane128 = lax.broadcasted_iota(jnp.int32, (1, SMALL_W), 1)
    fwd_lane = lane128 < LANE_DTB
    lane_lo = lane128 < SSM_HEAD_DIM
    p2 = p2_ref[...]
    zero_bf = jnp.zeros((), BF16)

    def chunk_a(c):
        rows = slice(c * C, (c + 1) * C)
        la = la_ref[rows, :]
        lg = lg_ref[rows, :]
        cs = _dot_exact_lhs(lmat, jnp.concatenate([lg, la], axis=1))
        ics, ics_b, af = cs[:, :GLA_QK], cs[:, GLA_QK:2 * GLA_QK], cs[:, 2 * GLA_QK:]
        rcs = ics_b[C - 1:C, :] - ics_b + lg[:, GLA_QK:]
        rb = af[C - 1:C, :] - af + la
        af_s[rows, :] = af
        rb_s[rows, :] = rb

        m_f = ics[C // 2 - 1:C // 2, :]
        tot_f = ics[C - 1:C, :]
        m_b = rcs[C // 2:C // 2 + 1, :]
        tot_b = rcs[0:1, :]
        q = qk_ref[rows, 0:GLA_QK].astype(F32)
        k = qk_ref[rows, GLA_QK:2 * GLA_QK].astype(F32)
        qf32 = q * jnp.exp2(ics - m_f)
        kf32 = k * jnp.exp2(m_f - ics)
        qb32 = q * jnp.exp2(rcs - m_b)
        kb32 = k * jnp.exp2(m_b - rcs)
        kfs = (kf32 * jnp.exp2(tot_f - m_f)).astype(BF16)
        kbs = (kb32 * jnp.exp2(tot_b - m_b)).astype(BF16)
        qf_s[rows, :] = qf32.astype(BF16)
        qb_s[rows, :] = qb32.astype(BF16)
        kf_s[rows, :] = kf32.astype(BF16)
        kb_s[rows, :] = kb32.astype(BF16)
        qfi_s[rows, :] = (qf32 * jnp.exp2(m_f)).astype(BF16)
        qbi_s[rows, :] = (qb32 * jnp.exp2(m_b)).astype(BF16)
        etf_s[c] = jnp.broadcast_to(jnp.exp2(tot_f), (8, GLA_QK))
        etb_s[c] = jnp.broadcast_to(jnp.exp2(tot_b), (8, GLA_QK))
        vt_kf = _dot_tn(v_ref[rows, :], kfs)
        vt_kb = _dot_tn(v_ref[rows, :], kbs)
        kvf = jnp.zeros((GLA_DV, GLA_QK), F32)
        kvb = jnp.zeros((GLA_DV, GLA_QK), F32)
        for h in range(GLA_HEADS):
            hm = head_of_lane == h
            kvf = kvf + jnp.where(hm, vt_kf[h * GLA_DV:(h + 1) * GLA_DV, :], 0.0)
            kvb = kvb + jnp.where(hm, vt_kb[h * GLA_DV:(h + 1) * GLA_DV, :], 0.0)
        kvf_s[c] = kvf
        kvb_s[c] = kvb

        tot_a = af[C - 1:C, :]
        tot_r = rb[0:1, :]
        sw = jnp.exp2(jnp.where(fwd_lane, tot_a - af, tot_r - rb) + ldt_ref[rows, :])
        return sw.astype(BF16), jnp.exp2(jnp.where(fwd_lane, tot_a, tot_r))

    def chunk_a2(c, swx):
        rows = slice(c * C, (c + 1) * C)
        xsf = xa_ref[rows, 0:SSM_INNER].astype(F32)
        bm = xa_ref[rows, SSM_INNER:SSM_INNER + SSM_BC]
        xwf = (xsf * swx[:, :SSM_INNER].astype(F32)).astype(BF16)
        xwb = (xsf * swx[:, SSM_INNER:].astype(F32)).astype(BF16)
        bms = [bm[:, g * SSM_STATE:(g + 1) * SSM_STATE] for g in range(SSM_GROUPS)]
        dsf_s[c] = jnp.concatenate(
            [_dot_tn(bms[g], xwf[:, g * 256:(g + 1) * 256]) for g in range(SSM_GROUPS)], axis=0)
        dsb_s[c] = jnp.concatenate(
            [_dot_tn(bms[g], xwb[:, g * 256:(g + 1) * 256]) for g in range(SSM_GROUPS)], axis=0)

    n_fin = 2 if nc % 2 == 0 else 1
    sws, ets = [], []
    for c in range(nc):
        sw_c, et_c = chunk_a(c)
        sws.append(sw_c)
        ets.append(et_c)
    swx_all = _dot(jnp.concatenate(sws, axis=0), p2).astype(BF16)
    etx = _dot_exact_rhs(jnp.concatenate(ets + [jnp.zeros((8 - nc % 8, SMALL_W), F32)] * (nc % 8 > 0), axis=0), p2)
    for c in range(nc):
        chunk_a2(c, swx_all[c * C:(c + 1) * C, :])
        decf_s[c] = _stack_row(etx[c:c + 1, :SSM_INNER])
        decb_s[c] = _stack_row(etx[c:c + 1, SSM_INNER:])
        if n_fin == 2 and c == nc // 2 - 1:
            finish(0, 2)

    sg = sgf_s[...]
    ss = ssf_s[...]
    for c in range(nc):
        stf_s[c] = sg.astype(BF16)
        ssf_c[c] = ss.astype(BF16)
        sg = sg * etf_s[c][0:1, :] + kvf_s[c]
        ss = ss * decf_s[c] + dsf_s[c]
    sgf_s[...] = sg
    ssf_s[...] = ss
    sg = sgb_ref[...]
    ss = ssb_ref[...]
    for c in reversed(range(nc)):
        stb_s[c] = sg.astype(BF16)
        ssb_c[c] = ss.astype(BF16)
        sg = sg * etb_s[c][0:1, :] + kvb_s[c]
        ss = ss * decb_s[c] + dsb_s[c]

    tri4 = jnp.concatenate([tri] * 4, axis=1)

    def chunk_o(c):
        rows = slice(c * C, (c + 1) * C)
        kf = kf_s[rows, :]
        kb = kb_s[rows, :]
        kf_m = jnp.concatenate([jnp.where(head_of_lane == h, kf, zero_bf) for h in range(GLA_HEADS)], axis=0)
        kb_m = jnp.concatenate([jnp.where(head_of_lane == h, kb, zero_bf) for h in range(GLA_HEADS)], axis=0)
        att = jnp.where(tri4, _dot_nt(qf_s[rows, :], kf_m), _dot_nt(qb_s[rows, :], kb_m)).astype(BF16)
        stf = stf_s[c]
        stb = stb_s[c]
        for h in range(GLA_HEADS):
            blk = slice((h // 2) * LANES, (h // 2 + 1) * LANES)
            hm = (lane128 // GLA_DK) == (h % 2)
            qi = jnp.concatenate([jnp.where(hm, qfi_s[rows, blk], zero_bf),
                                  jnp.where(hm, qbi_s[rows, blk], zero_bf)], axis=1)
            st = jnp.concatenate([stf[:, blk], stb[:, blk]], axis=1)
            vh = v_ref[rows, h * GLA_DV:(h + 1) * GLA_DV]
            o_s[slot_w, rows, h * GLA_DV:(h + 1) * GLA_DV] = _dot(att[:, h * C:(h + 1) * C], vh) + _dot_nt(qi, st)

        xs = xa_ref[rows, 0:SSM_INNER]
        bm = xa_ref[rows, SSM_INNER:SSM_INNER + SSM_BC]
        cm = xa_ref[rows, SSM_INNER + SSM_BC:SSM_XBC]
        af = af_s[rows, :]
        rb = rb_s[rows, :]
        ldt_t = ldt_ref[rows, :].T
        arow_t = af.T - ldt_t
        rrow_t = rb.T - ldt_t
        bm_m = jnp.concatenate([jnp.where(group_of_lane == g, bm, zero_bf) for g in range(SSM_GROUPS)], axis=0)
        cbm_all = _dot_nt(cm, bm_m)
        cm32 = cm.astype(F32)
        ssf = ssf_c[c]
        ssb = ssb_c[c]
        for g in range(SSM_GROUPS):
            cbm = cbm_all[:, g * C:(g + 1) * C]
            cblk = cm32[:, (g // 2) * LANES:(g // 2 + 1) * LANES]
            crot = pltpu.roll(cblk, SSM_STATE, 1)
            cm2 = jnp.where(lane_lo, cblk, crot) if g % 2 == 0 else jnp.where(lane_lo, crot, cblk)
            srow = slice(g * SSM_STATE, (g + 1) * SSM_STATE)
            for pair in range(SSM_HPG // 2):
                blk = (g * SSM_HPG + pair * 2) * SSM_HEAD_DIM
                pcol = slice(pair * LANES, (pair + 1) * LANES)
                rhs = jnp.concatenate([xs[:, blk:blk + LANES], ssf[srow, pcol], ssb[srow, pcol]], axis=0)
                lhs = []
                for jj in range(2):
                    hd = g * SSM_HPG + pair * 2 + jj
                    lf = LANE_DTF + hd
                    lb = LANE_DTB + hd
                    a_col = jnp.broadcast_to(af[:, lf:lf + 1], (C, C))
                    r_col = jnp.broadcast_to(rb[:, lb:lb + 1], (C, C))
                    diff = jnp.where(tri, a_col - arow_t[lf:lf + 1, :], r_col - rrow_t[lb:lb + 1, :])
                    mm = (cbm * jnp.exp2(diff)).astype(BF16)
                    ecol = jnp.exp2(jnp.where(lane_lo, a_col, r_col))
                    lhs.append(jnp.concatenate([mm, (cm2 * ecol).astype(BF16)], axis=1))
                yy = _dot(jnp.concatenate(lhs, axis=0), rhs)
                xpair = xs[:, blk:blk + LANES].astype(F32)
                o_s[slot_w, rows, GLA_V + blk:GLA_V + blk + LANES] = (
                    jnp.where(lane_lo, yy[0:C, :], yy[C:2 * C, :]) + dskip_ref[:, blk:blk + LANES] * xpair)

    for c in range(nc):
        chunk_o(c)
        if c == nc // 2 - 1 or nc == 1:
            finish(n_fin - 1, n_fin)


def _mixer(x, qk, v, sog, sz, xa, sgg, sgs, lg, la, ldt, sgb, ssb, consts, seq_len, t):
    n = x.shape[0]
    ns = seq_len // t
    nsteps = n // t
    nc = t // CHUNK
    cur = lambda g: (jnp.minimum(g, nsteps - 1), 0)
    lag = lambda g: (jnp.maximum(g - 1, 0), 0)
    tok = lambda w: pl.BlockSpec((t, w), cur)
    tok_lag = lambda w: pl.BlockSpec((t, w), lag)
    st = lambda r, c: pl.BlockSpec((None, r, c), lambda g: (jnp.minimum(g, nsteps - 1), 0, 0))
    scratch = [
        pltpu.VMEM((GLA_DV, GLA_QK), F32),
        pltpu.VMEM((SSM_BC, 256), F32),
        pltpu.VMEM((t, SMALL_W), F32),
        pltpu.VMEM((t, SMALL_W), F32),
        pltpu.VMEM((t, GLA_QK), BF16),
        pltpu.VMEM((t, GLA_QK), BF16),
        pltpu.VMEM((t, GLA_QK), BF16),
        pltpu.VMEM((t, GLA_QK), BF16),
        pltpu.VMEM((t, GLA_QK), BF16),
        pltpu.VMEM((t, GLA_QK), BF16),
        pltpu.VMEM((nc, 8, GLA_QK), F32),
        pltpu.VMEM((nc, 8, GLA_QK), F32),
        pltpu.VMEM((nc, GLA_DV, GLA_QK), F32),
        pltpu.VMEM((nc, GLA_DV, GLA_QK), F32),
        pltpu.VMEM((nc, GLA_DV, GLA_QK), BF16),
        pltpu.VMEM((nc, GLA_DV, GLA_QK), BF16),
        pltpu.VMEM((nc, SSM_BC, 256), F32),
        pltpu.VMEM((nc, SSM_BC, 256), F32),
        pltpu.VMEM((nc, SSM_BC, 256), F32),
        pltpu.VMEM((nc, SSM_BC, 256), F32),
        pltpu.VMEM((nc, SSM_BC, 256), BF16),
        pltpu.VMEM((nc, SSM_BC, 256), BF16),
        pltpu.VMEM((2, t, GLA_V + SSM_INNER), F32),
    ]
    return pl.pallas_call(
        functools.partial(_mixer_kernel, ns=ns),
        grid=(nsteps + 1,),
        in_specs=[tok_lag(D_MODEL), tok(2 * GLA_QK), tok(GLA_V), tok_lag(GLA_V), tok_lag(SSM_INNER), tok(SSM_XBC),
                  tok_lag(D_MODEL), tok_lag(D_MODEL), tok(2 * GLA_QK), tok(SMALL_W), tok(SMALL_W),
                  st(GLA_DV, GLA_QK), st(SSM_BC, 256)]
                 + [_const_spec(a.shape) for a in consts],
        out_specs=tok_lag(D_MODEL),
        out_shape=jax.ShapeDtypeStruct((n, D_MODEL), F32),
        scratch_shapes=scratch,
        compiler_params=pltpu.CompilerParams(dimension_semantics=("arbitrary",), vmem_limit_bytes=VMEM_LIMIT),
        name="mixer",
    )(x, qk, v, sog, sz, xa, sgg, sgs, lg, la, ldt, sgb, ssb, *consts)


def _ffn_kernel(x_ref, nw_ref, wg_ref, wu_ref, wd_ref, fw_ref, out_ref):
    x = x_ref[...]
    h = _rms(x, nw_ref[...]).astype(BF16)
    a = (_silu(_dot(h, wg_ref[...])) * _dot(h, wu_ref[...])).astype(BF16)
    y = x + _dot(a, wd_ref[...])
    out_ref[...] = _rms(y, fw_ref[...])


def _ffn(x2d, nw, wg, wu, wd, fw, tm):
    n = x2d.shape[0]
    row = lambda i: (i, 0)
    return pl.pallas_call(
        _ffn_kernel,
        grid=(n // tm,),
        in_specs=[pl.BlockSpec((tm, D_MODEL), row)] + [_const_spec(a.shape) for a in (nw, wg, wu, wd, fw)],
        out_specs=pl.BlockSpec((tm, D_MODEL), row),
        out_shape=jax.ShapeDtypeStruct((n, D_MODEL), F32),
        compiler_params=pltpu.CompilerParams(dimension_semantics=("arbitrary",), vmem_limit_bytes=VMEM_LIMIT),
        name="ffn",
    )(x2d, nw, wg, wu, wd, fw)


def _tile(n, pref):
    t = pref
    while n % t:
        t //= 2
    return t


def _prep_weights(norm_mix_w, w_in, gla_up_f, gla_bias_f, gla_up_b, gla_bias_b, gla_norm_w, conv_w, conv_b,
                  dt_bias_f, dt_bias_b, a_log_f, a_log_b, d_skip, ssm_norm_w, w_br_gla, w_br_ssm, w_out,
                  norm_ffn_w, w_ffn_gate, w_ffn_up, w_ffn_down, norm_final_w):
    offs = [0]
    for sz in IN_SIZES:
        offs.append(offs[-1] + sz)
    col = lambda i: w_in[:, offs[i]:offs[i + 1]]
    wq, wk, wv, wog, wrf, wrb, wz, wxbc, wdtf, wdtb, wgg, wgs = [col(i) for i in range(12)]
    wqk = jnp.concatenate([wq * (GLA_DK ** -0.5), wk], axis=1)
    wsm = jnp.concatenate([wrf, wrb, wdtf, wdtb, jnp.zeros((D_MODEL, SMALL_W - 64), F32)], axis=1)
    proj_ws = [w.astype(BF16) for w in (wqk, wv, 0.5 * wog, 0.5 * wz, wxbc, 0.5 * wgg, 0.5 * wgs, wsm)]

    upc = jnp.zeros((SMALL_W, 2 * GLA_QK), F32)
    upc = upc.at[LANE_RF:LANE_RF + GLA_GATE_RANK, :GLA_QK].set(gla_up_f)
    upc = upc.at[LANE_RB:LANE_RB + GLA_GATE_RANK, GLA_QK:].set(gla_up_b)
    upc = upc.astype(BF16)
    gbias = jnp.concatenate([gla_bias_f, gla_bias_b])[None, :]

    def small_row(f, b):
        r = jnp.zeros((1, SMALL_W), F32)
        r = r.at[0, LANE_DTF:LANE_DTF + SSM_HEADS].set(f)
        return r.at[0, LANE_DTB:LANE_DTB + SSM_HEADS].set(b)

    dtbias = small_row(dt_bias_f, dt_bias_b)
    alog = small_row(a_log_f, a_log_b)

    lane = jnp.arange(SMALL_W)[:, None]
    blk = jnp.arange(SSM_INNER)[None, :] // SSM_HEAD_DIM
    p2f = (lane == LANE_DTF + blk)
    p2b = (lane == LANE_DTB + blk)
    p2 = jnp.concatenate([p2f, p2b], axis=1).astype(BF16)
    p2b = p2b.astype(BF16)

    cwh = 0.5 * jnp.concatenate([conv_w, jnp.zeros((8 - SSM_CONV, SSM_XBC), F32)], axis=0)
    return dict(
        nw=norm_mix_w[None, :], proj_ws=proj_ws,
        proj_consts=(upc, gbias, dtbias, alog, cwh, 0.5 * conv_b[None, :]),
        p2=p2, p2b=p2b,
        glanw=jnp.tile(gla_norm_w, GLA_HEADS)[None, :], dskip=jnp.repeat(d_skip, SSM_HEAD_DIM)[None, :],
        ssmnw=ssm_norm_w[None, :], wbg=w_br_gla.astype(BF16), wbs=w_br_ssm.astype(BF16),
        wo=w_out.astype(BF16), fnw=norm_ffn_w[None, :], wg=w_ffn_gate.astype(BF16),
        wu=w_ffn_up.astype(BF16), wd=w_ffn_down.astype(BF16), finw=norm_final_w[None, :])


def _run(x, p):
    b, l, _ = x.shape
    n = b * l
    t = _tile(l, 512)
    x2d = x.reshape(n, D_MODEL)
    qk, v, sog, sz, xa, sgg, sgs, lg, la, ldt = _inproj(x2d, l, p["nw"], p["proj_ws"], p["proj_consts"], t)
    r3 = lambda a: a.reshape(b, l, a.shape[-1])
    sgb, ssb = _bwd_states(r3(qk), r3(v), r3(xa), r3(lg), r3(la), r3(ldt), p["p2b"], t)
    sgb = sgb.reshape(n // t, GLA_DV, GLA_QK)
    ssb = ssb.reshape(n // t, SSM_BC, 256)
    consts = (p["glanw"], p["dskip"], p["ssmnw"], p["p2"], p["wbg"], p["wbs"], p["wo"])
    xm = _mixer(x2d, qk, v, sog, sz, xa, sgg, sgs, lg, la, ldt, sgb, ssb, consts, l, t)
    y = _ffn(xm, p["fnw"], p["wg"], p["wu"], p["wd"], p["finw"], _tile(n, 512))
    return y.reshape(b, l, D_MODEL)


def kernel(x_prompt, x_sample, norm_mix_w, w_in, gla_up_f, gla_bias_f, gla_up_b, gla_bias_b, gla_norm_w, conv_w,
           conv_b, dt_bias_f, dt_bias_b, a_log_f, a_log_b, d_skip, ssm_norm_w, w_br_gla, w_br_ssm, w_out,
           norm_ffn_w, w_ffn_gate, w_ffn_up, w_ffn_down, norm_final_w):
    assert norm_mix_w.shape[0] == 1, "single-layer block"
    p = _prep_weights(norm_mix_w[0], w_in[0], gla_up_f[0], gla_bias_f[0], gla_up_b[0], gla_bias_b[0],
                      gla_norm_w[0], conv_w[0], conv_b[0], dt_bias_f[0], dt_bias_b[0], a_log_f[0], a_log_b[0],
                      d_skip[0], ssm_norm_w[0], w_br_gla[0], w_br_ssm[0], w_out[0], norm_ffn_w[0],
                      w_ffn_gate[0], w_ffn_up[0], w_ffn_down[0], norm_final_w)
    return (_run(x_prompt, p), _run(x_sample, p))
```

```python
import functools
import math

import jax
import jax.numpy as jnp
from jax import lax
from jax.experimental import pallas as pl
from jax.experimental.pallas import tpu as pltpu

F32 = jnp.float32
BF16 = jnp.bfloat16

D_MODEL = 1024
GLA_HEADS = 4
GLA_DK = 64
GLA_DV = 128
GLA_QK = GLA_HEADS * GLA_DK
GLA_V = GLA_HEADS * GLA_DV
GLA_GATE_RANK = 16
GLA_GATE_NORM = 16.0
SSM_HEAD_DIM = 64
SSM_INNER = 1024
SSM_HEADS = 16
SSM_GROUPS = 4
SSM_HPG = 4
SSM_STATE = 64
SSM_BC = SSM_GROUPS * SSM_STATE
SSM_CONV = 5
SSM_XBC = SSM_INNER + 2 * SSM_BC
D_FF = 2816
EPS = 1e-6
LOG2E = math.log2(math.e)
IN_SIZES = (GLA_QK, GLA_QK, GLA_V, GLA_V, GLA_GATE_RANK, GLA_GATE_RANK,
            SSM_INNER, SSM_XBC, SSM_HEADS, SSM_HEADS, D_MODEL, D_MODEL)

SMALL_W = 128
LANE_RF = 0
LANE_RB = 16
LANE_DTF = 32
LANE_DTB = 48

LANES = 128
CHUNK = 128
HALO = 8
VMEM_LIMIT = 56 * 1024 * 1024


def _dot(a, b):
    return jnp.dot(a, b, preferred_element_type=F32)


def _dot_nt(a, b):
    return lax.dot_general(a, b, (((1,), (1,)), ((), ())), preferred_element_type=F32)


def _dot_tn(a, b):
    return lax.dot_general(a, b, (((0,), (0,)), ((), ())), preferred_element_type=F32)


def _split2(x):
    hi = x.astype(BF16)
    return hi, (x - hi.astype(F32)).astype(BF16)


def _dot_exact_lhs(m, x):
    hi, lo = _split2(x)
    return _dot(m, hi) + _dot(m, lo)


def _dot_exact_rhs(x, m):
    hi, lo = _split2(x)
    return _dot(hi, m) + _dot(lo, m)


def _rms(x, w):
    return x * lax.rsqrt(jnp.mean(x * x, axis=-1, keepdims=True) + EPS) * w


def _sigmoid_of_half(h):
    return 0.5 + 0.5 * jnp.tanh(h)


def _silu_of_half(h):
    return h + h * jnp.tanh(h)


def _silu(x):
    return _silu_of_half(0.5 * x)


def _softplus(x):
    return jnp.maximum(x, 0.0) + jnp.log1p(jnp.exp(-jnp.abs(x)))


def _tri_mask(n):
    row = lax.broadcasted_iota(jnp.int32, (n, n), 0)
    col = lax.broadcasted_iota(jnp.int32, (n, n), 1)
    return col <= row


def _lane_block(width, block):
    return lax.broadcasted_iota(jnp.int32, (1, width), 1) // block


def _stack_row(row):
    return jnp.concatenate(
        [jnp.broadcast_to(row[:, g * 256:(g + 1) * 256], (SSM_STATE, 256)) for g in range(SSM_GROUPS)], axis=0)


def _inproj_kernel(tiles_per_row, x_ref, xp_ref, xn_ref, nw_ref,
                   wqk_ref, wv_ref, wog_ref, wz_ref, wxbc_ref, wgg_ref, wgs_ref, wsm_ref,
                   upc_ref, gbias_ref, dtbias_ref, alog_ref, cwh_ref, cbh_ref,
                   qk_ref, v_ref, sog_ref, sz_ref, xa_ref, sgg_ref, sgs_ref, lg_ref, la_ref, ldt_ref,
                   xp_s):
    i = pl.program_id(0)
    tm = x_ref.shape[0]
    nw = nw_ref[...]
    u = _rms(x_ref[...], nw).astype(BF16)

    pos = i % tiles_per_row
    has_prev = jnp.where(pos > 0, 1.0, 0.0)
    has_next = jnp.where(pos < tiles_per_row - 1, 1.0, 0.0)
    xh = jnp.concatenate([xp_ref[...], xn_ref[...]], axis=0)
    xbc_h = _dot(_rms(xh, nw).astype(BF16), wxbc_ref[...])
    xbc = _dot(u, wxbc_ref[...])
    n_slab = SSM_XBC // LANES
    for j in range(n_slab):
        cols = slice(j * LANES, (j + 1) * LANES)
        xp_s[j, HALO:HALO + tm, :] = xbc[:, cols]
        xp_s[j, 0:HALO, :] = xbc_h[0:HALO, cols] * has_prev
        xp_s[j, HALO + tm:2 * HALO + tm, :] = xbc_h[HALO:2 * HALO, cols] * has_next

    def conv_slab(j):
        cols = slice(j * LANES, (j + 1) * LANES)
        acc = cbh_ref[:, cols] + cwh_ref[0:1, cols] * xp_s[j, HALO - SSM_CONV // 2:HALO - SSM_CONV // 2 + tm, :]
        for tap in range(1, SSM_CONV):
            off = HALO + tap - SSM_CONV // 2
            acc = acc + cwh_ref[tap:tap + 1, cols] * xp_s[j, off:off + tm, :]
        xa_ref[:, cols] = _silu_of_half(acc).astype(BF16)

    sm = _dot(u, wsm_ref[...])
    xg = _dot(sm.astype(BF16), upc_ref[...]) + gbias_ref[...]
    lg_ref[...] = (jnp.minimum(xg, 0.0) * (LOG2E / GLA_GATE_NORM)
                   - jnp.log2(1.0 + jnp.exp2(jnp.abs(xg) * (-LOG2E))) * (1.0 / GLA_GATE_NORM))
    dt = _softplus(sm + dtbias_ref[...])
    lane = lax.broadcasted_iota(jnp.int32, (1, SMALL_W), 1)
    live = (lane >= LANE_DTF) & (lane < LANE_DTB + SSM_HEADS)
    la_ref[...] = dt * jnp.where(live, -LOG2E * jnp.exp(alog_ref[...]), 0.0)
    ldt_ref[...] = jnp.log2(dt)

    groups = [
        lambda: qk_ref.__setitem__(..., _dot(u, wqk_ref[...]).astype(BF16)),
        lambda: v_ref.__setitem__(..., _dot(u, wv_ref[...]).astype(BF16)),
        lambda: sog_ref.__setitem__(..., _silu_of_half(_dot(u, wog_ref[...])).astype(BF16)),
        lambda: sz_ref.__setitem__(..., _silu_of_half(_dot(u, wz_ref[...])).astype(BF16)),
        lambda: sgg_ref.__setitem__(..., _sigmoid_of_half(_dot(u, wgg_ref[...])).astype(BF16)),
        lambda: sgs_ref.__setitem__(..., _sigmoid_of_half(_dot(u, wgs_ref[...])).astype(BF16)),
    ]
    per = n_slab // len(groups)
    for gi, emit in enumerate(groups):
        for j in range(gi * per, (gi + 1) * per):
            conv_slab(j)
        emit()


def _const_spec(shape):
    nd = len(shape)
    return pl.BlockSpec(shape, lambda *_: (0,) * nd, pipeline_mode=pl.Buffered(1))


def _inproj(x2d, seq_len, nw, ws, consts, tm):
    n = x2d.shape[0]
    hb = tm // HALO
    nh = n // HALO
    widths = [2 * GLA_QK, GLA_V, GLA_V, SSM_INNER, SSM_XBC, D_MODEL, D_MODEL, 2 * GLA_QK, SMALL_W, SMALL_W]
    dtypes = [BF16] * 7 + [F32] * 3
    row = lambda i: (i, 0)
    return pl.pallas_call(
        functools.partial(_inproj_kernel, seq_len // tm),
        grid=(n // tm,),
        in_specs=[pl.BlockSpec((tm, D_MODEL), row),
                  pl.BlockSpec((HALO, D_MODEL), lambda i: (jnp.maximum(i * hb - 1, 0), 0)),
                  pl.BlockSpec((HALO, D_MODEL), lambda i: (jnp.minimum((i + 1) * hb, nh - 1), 0)),
                  _const_spec(nw.shape)]
                 + [_const_spec(w.shape) for w in ws] + [_const_spec(a.shape) for a in consts],
        out_specs=[pl.BlockSpec((tm, w), row) for w in widths],
        out_shape=[jax.ShapeDtypeStruct((n, w), dt) for w, dt in zip(widths, dtypes)],
        scratch_shapes=[pltpu.VMEM((SSM_XBC // LANES, tm + 2 * HALO, LANES), F32)],
        compiler_params=pltpu.CompilerParams(dimension_semantics=("arbitrary",), vmem_limit_bytes=VMEM_LIMIT),
        name="inproj",
    )(x2d, x2d, x2d, nw, *ws, *consts)


def _bwd_state_kernel(k_ref, v_ref, xa_ref, lgb_ref, la_ref, ldt_ref, p2b_ref,
                      sgb_ref, ssb_ref,
                      sg_s, ss_s):
    s = pl.program_id(1)
    t = k_ref.shape[0]
    nc = t // CHUNK

    @pl.when(s == 0)
    def _():
        sg_s[...] = jnp.zeros_like(sg_s)
        ss_s[...] = jnp.zeros_like(ss_s)

    sgb_ref[...] = sg_s[...]
    ssb_ref[...] = ss_s[...]

    zz = jnp.concatenate([lgb_ref[...], la_ref[...]], axis=1)
    lmat = jnp.where(_tri_mask(CHUNK), 1.0, 0.0).astype(BF16)
    carry = jnp.zeros((1, GLA_QK + SMALL_W), F32)
    parts = []
    for c in range(nc):
        cs = _dot_exact_lhs(lmat, zz[c * CHUNK:(c + 1) * CHUNK]) + carry
        parts.append(cs)
        carry = cs[CHUNK - 1:CHUNK, :]
    ics = jnp.concatenate(parts, axis=0)
    tot = carry
    ecs = ics - zz

    kbs = (k_ref[...].astype(F32) * jnp.exp2(ecs[:, :GLA_QK])).astype(BF16)
    head_of_lane = _lane_block(GLA_QK, GLA_DK)
    new_g = sg_s[...] * jnp.exp2(tot[:, :GLA_QK])
    vt_k = _dot_tn(v_ref[...], kbs)
    for h in range(GLA_HEADS):
        new_g = new_g + jnp.where(head_of_lane == h, vt_k[h * GLA_DV:(h + 1) * GLA_DV, :], 0.0)
    sg_s[...] = new_g

    p2b = p2b_ref[...]
    wb = jnp.exp2(ecs[:, GLA_QK:] + ldt_ref[...])
    wexp = _dot(wb.astype(BF16), p2b)
    xw = (xa_ref[:, 0:SSM_INNER].astype(F32) * wexp).astype(BF16)
    contrib = jnp.concatenate(
        [_dot_tn(xa_ref[:, SSM_INNER + g * SSM_STATE:SSM_INNER + (g + 1) * SSM_STATE], xw[:, g * 256:(g + 1) * 256])
         for g in range(SSM_GROUPS)], axis=0)
    et = jnp.broadcast_to(jnp.exp2(tot[:, GLA_QK:]), (8, SMALL_W))
    dec = _stack_row(_dot_exact_rhs(et, p2b)[0:1, :])
    ss_s[...] = ss_s[...] * dec + contrib


def _bwd_states(qk, v, xa, lg, la, ldt, p2b, t):
    b, l, _ = qk.shape
    ns = l // t
    rev = lambda i, s: (i, ns - 1 - s, 0)
    rev1 = lambda i, s: (i, ns - 1 - s, 1)
    return pl.pallas_call(
        _bwd_state_kernel,
        grid=(b, ns),
        in_specs=[
            pl.BlockSpec((None, t, GLA_QK), rev1),
            pl.BlockSpec((None, t, GLA_V), rev),
            pl.BlockSpec((None, t, SSM_INNER + SSM_BC), rev),
            pl.BlockSpec((None, t, GLA_QK), rev1),
            pl.BlockSpec((None, t, SMALL_W), rev),
            pl.BlockSpec((None, t, SMALL_W), rev),
            _const_spec(p2b.shape),
        ],
        out_specs=[
            pl.BlockSpec((None, None, GLA_DV, GLA_QK), lambda i, s: (i, ns - 1 - s, 0, 0)),
            pl.BlockSpec((None, None, SSM_BC, 256), lambda i, s: (i, ns - 1 - s, 0, 0)),
        ],
        out_shape=[
            jax.ShapeDtypeStruct((b, ns, GLA_DV, GLA_QK), F32),
            jax.ShapeDtypeStruct((b, ns, SSM_BC, 256), F32),
        ],
        scratch_shapes=[pltpu.VMEM((GLA_DV, GLA_QK), F32), pltpu.VMEM((SSM_BC, 256), F32)],
        compiler_params=pltpu.CompilerParams(dimension_semantics=("arbitrary", "arbitrary"),
                                             vmem_limit_bytes=VMEM_LIMIT),
        name="bwd_states",
    )(qk, v, xa, lg, la, ldt, p2b)


def _mixer_kernel(x_ref, qk_ref, v_ref, sog_ref, sz_ref, xa_ref, sgg_ref, sgs_ref, lg_ref, la_ref, ldt_ref,
                  sgb_ref, ssb_ref,
                  glanw_ref, dskip_ref, ssmnw_ref, p2_ref, wbg_ref, wbs_ref, wo_ref,
                  out_ref,
                  sgf_s, ssf_s, af_s, rb_s, qf_s, qb_s, kf_s, kb_s, qfi_s, qbi_s, etf_s, etb_s,
                  kvf_s, kvb_s, stf_s, stb_s, dsf_s, dsb_s, decf_s, decb_s, ssf_c, ssb_c, o_s, *, ns):
    g = pl.program_id(0)
    t = x_ref.shape[0]
    nc = t // CHUNK
    C = CHUNK

    @pl.when(g % ns == 0)
    def _():
        sgf_s[...] = jnp.zeros_like(sgf_s)
        ssf_s[...] = jnp.zeros_like(ssf_s)

    @pl.when(g == 0)
    def _():
        o_s[...] = jnp.zeros_like(o_s)

    slot_w = g % 2
    slot_r = 1 - slot_w

    def finish(part, nparts):
        rr = slice(part * (t // nparts), (part + 1) * (t // nparts))
        parts = []
        for h in range(GLA_HEADS):
            oh = o_s[slot_r, rr, h * GLA_DV:(h + 1) * GLA_DV]
            parts.append(oh * lax.rsqrt(jnp.mean(oh * oh, axis=-1, keepdims=True) + EPS))
        gl = (jnp.concatenate(parts, axis=1) * glanw_ref[...] * sog_ref[rr, :].astype(F32)).astype(BF16)
        y = _rms(o_s[slot_r, rr, GLA_V:] * sz_ref[rr, :].astype(F32), ssmnw_ref[...]).astype(BF16)
        m = (sgg_ref[rr, :].astype(F32) * _dot(gl, wbg_ref[...])
             + sgs_ref[rr, :].astype(F32) * _dot(y, wbs_ref[...]))
        out_ref[rr, :] = x_ref[rr, :] + _dot(m.astype(BF16), wo_ref[...])

    tri = _tri_mask(C)
    lmat = jnp.where(tri, 1.0, 0.0).astype(BF16)
    head_of_lane = _lane_block(GLA_QK, GLA_DK)
    group_of_lane = _lane_block(SSM_BC, SSM_STATE)
    lane128 = lax.broadcasted_iota(jnp.int32, (1, SMALL_W), 1)
    fwd_lane = lane128 < LANE_DTB
    lane_lo = lane128 < SSM_HEAD_DIM
    p2 = p2_ref[...]
    zero_bf = jnp.zeros((), BF16)

    def chunk_a(c):
        rows = slice(c * C, (c + 1) * C)
        la = la_ref[rows, :]
        lg = lg_ref[rows, :]
        cs = _dot_exact_lhs(lmat, jnp.concatenate([lg, la], axis=1))
        ics, ics_b, af = cs[:, :GLA_QK], cs[:, GLA_QK:2 * GLA_QK], cs[:, 2 * GLA_QK:]
        rcs = ics_b[C - 1:C, :] - ics_b + lg[:, GLA_QK:]
        rb = af[C - 1:C, :] - af + la
        af_s[rows, :] = af
        rb_s[rows, :] = rb

        m_f = ics[C // 2 - 1:C // 2, :]
        tot_f = ics[C - 1:C, :]
        m_b = rcs[C // 2:C // 2 + 1, :]
        tot_b = rcs[0:1, :]
        q = qk_ref[rows, 0:GLA_QK].astype(F32)
        k = qk_ref[rows, GLA_QK:2 * GLA_QK].astype(F32)
        qf32 = q * jnp.exp2(ics - m_f)
        kf32 = k * jnp.exp2(m_f - ics)
        qb32 = q * jnp.exp2(rcs - m_b)
        kb32 = k * jnp.exp2(m_b - rcs)
        kfs = (kf32 * jnp.exp2(tot_f - m_f)).astype(BF16)
        kbs = (kb32 * jnp.exp2(tot_b - m_b)).astype(BF16)
        qf_s[rows, :] = qf32.astype(BF16)
        qb_s[rows, :] = qb32.astype(BF16)
        kf_s[rows, :] = kf32.astype(BF16)
        kb_s[rows, :] = kb32.astype(BF16)
        qfi_s[rows, :] = (qf32 * jnp.exp2(m_f)).astype(BF16)
        qbi_s[rows, :] = (qb32 * jnp.exp2(m_b)).astype(BF16)
        etf_s[c] = jnp.broadcast_to(jnp.exp2(tot_f), (8, GLA_QK))
        etb_s[c] = jnp.broadcast_to(jnp.exp2(tot_b), (8, GLA_QK))
        vt_kf = _dot_tn(v_ref[rows, :], kfs)
        vt_kb = _dot_tn(v_ref[rows, :], kbs)
        kvf = jnp.zeros((GLA_DV, GLA_QK), F32)
        kvb = jnp.zeros((GLA_DV, GLA_QK), F32)
        for h in range(GLA_HEADS):
            hm = head_of_lane == h
            kvf = kvf + jnp.where(hm, vt_kf[h * GLA_DV:(h + 1) * GLA_DV, :], 0.0)
            kvb = kvb + jnp.where(hm, vt_kb[h * GLA_DV:(h + 1) * GLA_DV, :], 0.0)
        kvf_s[c] = kvf
        kvb_s[c] = kvb

        tot_a = af[C - 1:C, :]
        tot_r = rb[0:1, :]
        sw = jnp.exp2(jnp.where(fwd_lane, tot_a - af, tot_r - rb) + ldt_ref[rows, :])
        return sw.astype(BF16), jnp.exp2(jnp.where(fwd_lane, tot_a, tot_r))

    def chunk_a2(c, swx):
        rows = slice(c * C, (c + 1) * C)
        xsf = xa_ref[rows, 0:SSM_INNER].astype(F32)
        bm = xa_ref[rows, SSM_INNER:SSM_INNER + SSM_BC]
        xwf = (xsf * swx[:, :SSM_INNER].astype(F32)).astype(BF16)
        xwb = (xsf * swx[:, SSM_INNER:].astype(F32)).astype(BF16)
        bms = [bm[:, g * SSM_STATE:(g + 1) * SSM_STATE] for g in range(SSM_GROUPS)]
        dsf_s[c] = jnp.concatenate(
            [_dot_tn(bms[g], xwf[:, g * 256:(g + 1) * 256]) for g in range(SSM_GROUPS)], axis=0)
        dsb_s[c] = jnp.concatenate(
            [_dot_tn(bms[g], xwb[:, g * 256:(g + 1) * 256]) for g in range(SSM_GROUPS)], axis=0)

    n_fin = 2 if nc % 2 == 0 else 1
    sws, ets = [], []
    for c in range(nc):
        sw_c, et_c = chunk_a(c)
        sws.append(sw_c)
        ets.append(et_c)
    swx_all = _dot(jnp.concatenate(sws, axis=0), p2).astype(BF16)
    etx = _dot_exact_rhs(jnp.concatenate(ets + [jnp.zeros((8 - nc % 8, SMALL_W), F32)] * (nc % 8 > 0), axis=0), p2)
    for c in range(nc):
        chunk_a2(c, swx_all[c * C:(c + 1) * C, :])
        decf_s[c] = _stack_row(etx[c:c + 1, :SSM_INNER])
        decb_s[c] = _stack_row(etx[c:c + 1, SSM_INNER:])
        if n_fin == 2 and c == nc - 1:
            finish(0, 2)

    sg = sgf_s[...]
    ss = ssf_s[...]
    for c in range(nc):
        stf_s[c] = sg.astype(BF16)
        ssf_c[c] = ss.astype(BF16)
        sg = sg * etf_s[c][0:1, :] + kvf_s[c]
        ss = ss * decf_s[c] + dsf_s[c]
    sgf_s[...] = sg
    ssf_s[...] = ss
    sg = sgb_ref[...]
    ss = ssb_ref[...]
    for c in reversed(range(nc)):
        stb_s[c] = sg.astype(BF16)
        ssb_c[c] = ss.astype(BF16)
        sg = sg * etb_s[c][0:1, :] + kvb_s[c]
        ss = ss * decb_s[c] + dsb_s[c]

    tri4 = jnp.concatenate([tri] * 4, axis=1)

    def chunk_o(c):
        rows = slice(c * C, (c + 1) * C)
        kf = kf_s[rows, :]
        kb = kb_s[rows, :]
        kf_m = jnp.concatenate([jnp.where(head_of_lane == h, kf, zero_bf) for h in range(GLA_HEADS)], axis=0)
        kb_m = jnp.concatenate([jnp.where(head_of_lane == h, kb, zero_bf) for h in range(GLA_HEADS)], axis=0)
        att = jnp.where(tri4, _dot_nt(qf_s[rows, :], kf_m), _dot_nt(qb_s[rows, :], kb_m)).astype(BF16)
        stf = stf_s[c]
        stb = stb_s[c]
        for h in range(GLA_HEADS):
            blk = slice((h // 2) * LANES, (h // 2 + 1) * LANES)
            hm = (lane128 // GLA_DK) == (h % 2)
            qi = jnp.concatenate([jnp.where(hm, qfi_s[rows, blk], zero_bf),
                                  jnp.where(hm, qbi_s[rows, blk], zero_bf)], axis=1)
            st = jnp.concatenate([stf[:, blk], stb[:, blk]], axis=1)
            vh = v_ref[rows, h * GLA_DV:(h + 1) * GLA_DV]
            o_s[slot_w, rows, h * GLA_DV:(h + 1) * GLA_DV] = _dot(att[:, h * C:(h + 1) * C], vh) + _dot_nt(qi, st)

        xs = xa_ref[rows, 0:SSM_INNER]
        bm = xa_ref[rows, SSM_INNER:SSM_INNER + SSM_BC]
        cm = xa_ref[rows, SSM_INNER + SSM_BC:SSM_XBC]
        af = af_s[rows, :]
        rb = rb_s[rows, :]
        ldt_t = ldt_ref[rows, :].T
        arow_t = af.T - ldt_t
        rrow_t = rb.T - ldt_t
        bm_m = jnp.concatenate([jnp.where(group_of_lane == g, bm, zero_bf) for g in range(SSM_GROUPS)], axis=0)
        cbm_all = _dot_nt(cm, bm_m)
        cm32 = cm.astype(F32)
        ssf = ssf_c[c]
        ssb = ssb_c[c]
        for g in range(SSM_GROUPS):
            cbm = cbm_all[:, g * C:(g + 1) * C]
            cblk = cm32[:, (g // 2) * LANES:(g // 2 + 1) * LANES]
            crot = pltpu.roll(cblk, SSM_STATE, 1)
            cm2 = jnp.where(lane_lo, cblk, crot) if g % 2 == 0 else jnp.where(lane_lo, crot, cblk)
            srow = slice(g * SSM_STATE, (g + 1) * SSM_STATE)
            for pair in range(SSM_HPG // 2):
                blk = (g * SSM_HPG + pair * 2) * SSM_HEAD_DIM
                pcol = slice(pair * LANES, (pair + 1) * LANES)
                rhs = jnp.concatenate([xs[:, blk:blk + LANES], ssf[srow, pcol], ssb[srow, pcol]], axis=0)
                lhs = []
                for jj in range(2):
                    hd = g * SSM_HPG + pair * 2 + jj
                    lf = LANE_DTF + hd
                    lb = LANE_DTB + hd
                    a_col = jnp.broadcast_to(af[:, lf:lf + 1], (C, C))
                    r_col = jnp.broadcast_to(rb[:, lb:lb + 1], (C, C))
                    diff = jnp.where(tri, a_col - arow_t[lf:lf + 1, :], r_col - rrow_t[lb:lb + 1, :])
                    mm = (cbm * jnp.exp2(diff)).astype(BF16)
                    ecol = jnp.exp2(jnp.where(lane_lo, a_col, r_col))
                    lhs.append(jnp.concatenate([mm, (cm2 * ecol).astype(BF16)], axis=1))
                yy = _dot(jnp.concatenate(lhs, axis=0), rhs)
                xpair = xs[:, blk:blk + LANES].astype(F32)
                o_s[slot_w, rows, GLA_V + blk:GLA_V + blk + LANES] = (
                    jnp.where(lane_lo, yy[0:C, :], yy[C:2 * C, :]) + dskip_ref[:, blk:blk + LANES] * xpair)

    for c in range(nc):
        chunk_o(c)
        if c == nc // 2 - 1 or nc == 1:
            finish(n_fin - 1, n_fin)


def _mixer(x, qk, v, sog, sz, xa, sgg, sgs, lg, la, ldt, sgb, ssb, consts, seq_len, t):
    n = x.shape[0]
    ns = seq_len // t
    nsteps = n // t
    nc = t // CHUNK
    cur = lambda g: (jnp.minimum(g, nsteps - 1), 0)
    lag = lambda g: (jnp.maximum(g - 1, 0), 0)
    tok = lambda w: pl.BlockSpec((t, w), cur)
    tok_lag = lambda w: pl.BlockSpec((t, w), lag)
    st = lambda r, c: pl.BlockSpec((None, r, c), lambda g: (jnp.minimum(g, nsteps - 1), 0, 0))
    scratch = [
        pltpu.VMEM((GLA_DV, GLA_QK), F32),
        pltpu.VMEM((SSM_BC, 256), F32),
        pltpu.VMEM((t, SMALL_W), F32),
        pltpu.VMEM((t, SMALL_W), F32),
        pltpu.VMEM((t, GLA_QK), BF16),
        pltpu.VMEM((t, GLA_QK), BF16),
        pltpu.VMEM((t, GLA_QK), BF16),
        pltpu.VMEM((t, GLA_QK), BF16),
        pltpu.VMEM((t, GLA_QK), BF16),
        pltpu.VMEM((t, GLA_QK), BF16),
        pltpu.VMEM((nc, 8, GLA_QK), F32),
        pltpu.VMEM((nc, 8, GLA_QK), F32),
        pltpu.VMEM((nc, GLA_DV, GLA_QK), F32),
        pltpu.VMEM((nc, GLA_DV, GLA_QK), F32),
        pltpu.VMEM((nc, GLA_DV, GLA_QK), BF16),
        pltpu.VMEM((nc, GLA_DV, GLA_QK), BF16),
        pltpu.VMEM((nc, SSM_BC, 256), F32),
        pltpu.VMEM((nc, SSM_BC, 256), F32),
        pltpu.VMEM((nc, SSM_BC, 256), F32),
        pltpu.VMEM((nc, SSM_BC, 256), F32),
        pltpu.VMEM((nc, SSM_BC, 256), BF16),
        pltpu.VMEM((nc, SSM_BC, 256), BF16),
        pltpu.VMEM((2, t, GLA_V + SSM_INNER), F32),
    ]
    return pl.pallas_call(
        functools.partial(_mixer_kernel, ns=ns),
        grid=(nsteps + 1,),
        in_specs=[tok_lag(D_MODEL), tok(2 * GLA_QK), tok(GLA_V), tok_lag(GLA_V), tok_lag(SSM_INNER), tok(SSM_XBC),
                  tok_lag(D_MODEL), tok_lag(D_MODEL), tok(2 * GLA_QK), tok(SMALL_W), tok(SMALL_W),
                  st(GLA_DV, GLA_QK), st(SSM_BC, 256)]
                 + [_const_spec(a.shape) for a in consts],
        out_specs=tok_lag(D_MODEL),
        out_shape=jax.ShapeDtypeStruct((n, D_MODEL), F32),
        scratch_shapes=scratch,
        compiler_params=pltpu.CompilerParams(dimension_semantics=("arbitrary",), vmem_limit_bytes=VMEM_LIMIT),
        name="mixer",
    )(x, qk, v, sog, sz, xa, sgg, sgs, lg, la, ldt, sgb, ssb, *consts)


def _ffn_kernel(x_ref, nw_ref, wg_ref, wu_ref, wd_ref, fw_ref, out_ref):
    x = x_ref[...]
    h = _rms(x, nw_ref[...]).astype(BF16)
    a = (_silu(_dot(h, wg_ref[...])) * _dot(h, wu_ref[...])).astype(BF16)
    y = x + _dot(a, wd_ref[...])
    out_ref[...] = _rms(y, fw_ref[...])


def _ffn(x2d, nw, wg, wu, wd, fw, tm):
    n = x2d.shape[0]
    row = lambda i: (i, 0)
    return pl.pallas_call(
        _ffn_kernel,
        grid=(n // tm,),
        in_specs=[pl.BlockSpec((tm, D_MODEL), row)] + [_const_spec(a.shape) for a in (nw, wg, wu, wd, fw)],
        out_specs=pl.BlockSpec((tm, D_MODEL), row),
        out_shape=jax.ShapeDtypeStruct((n, D_MODEL), F32),
        compiler_params=pltpu.CompilerParams(dimension_semantics=("arbitrary",), vmem_limit_bytes=VMEM_LIMIT),
        name="ffn",
    )(x2d, nw, wg, wu, wd, fw)


def _tile(n, pref):
    t = pref
    while n % t:
        t //= 2
    return t


def _prep_weights(norm_mix_w, w_in, gla_up_f, gla_bias_f, gla_up_b, gla_bias_b, gla_norm_w, conv_w, conv_b,
                  dt_bias_f, dt_bias_b, a_log_f, a_log_b, d_skip, ssm_norm_w, w_br_gla, w_br_ssm, w_out,
                  norm_ffn_w, w_ffn_gate, w_ffn_up, w_ffn_down, norm_final_w):
    offs = [0]
    for sz in IN_SIZES:
        offs.append(offs[-1] + sz)
    w_in = w_in.astype(BF16)
    col = lambda i: w_in[:, offs[i]:offs[i + 1]]
    wq, wk, wv, wog, wrf, wrb, wz, wxbc, wdtf, wdtb, wgg, wgs = [col(i) for i in range(12)]
    wqk = jnp.concatenate([wq * (GLA_DK ** -0.5), wk], axis=1)
    wsm = jnp.concatenate([wrf, wrb, wdtf, wdtb, jnp.zeros((D_MODEL, SMALL_W - 64), BF16)], axis=1)
    proj_ws = [w.astype(BF16) for w in (wqk, wv, 0.5 * wog, 0.5 * wz, wxbc, 0.5 * wgg, 0.5 * wgs, wsm)]

    upc = jnp.zeros((SMALL_W, 2 * GLA_QK), F32)
    upc = upc.at[LANE_RF:LANE_RF + GLA_GATE_RANK, :GLA_QK].set(gla_up_f)
    upc = upc.at[LANE_RB:LANE_RB + GLA_GATE_RANK, GLA_QK:].set(gla_up_b)
    upc = upc.astype(BF16)
    gbias = jnp.concatenate([gla_bias_f, gla_bias_b])[None, :]

    def small_row(f, b):
        r = jnp.zeros((1, SMALL_W), F32)
        r = r.at[0, LANE_DTF:LANE_DTF + SSM_HEADS].set(f)
        return r.at[0, LANE_DTB:LANE_DTB + SSM_HEADS].set(b)

    dtbias = small_row(dt_bias_f, dt_bias_b)
    alog = small_row(a_log_f, a_log_b)

    lane = jnp.arange(SMALL_W)[:, None]
    blk = jnp.arange(SSM_INNER)[None, :] // SSM_HEAD_DIM
    p2f = (lane == LANE_DTF + blk)
    p2b = (lane == LANE_DTB + blk)
    p2 = jnp.concatenate([p2f, p2b], axis=1).astype(BF16)
    p2b = p2b.astype(BF16)

    cwh = 0.5 * jnp.concatenate([conv_w, jnp.zeros((8 - SSM_CONV, SSM_XBC), F32)], axis=0)
    return dict(
        nw=norm_mix_w[None, :], proj_ws=proj_ws,
        proj_consts=(upc, gbias, dtbias, alog, cwh, 0.5 * conv_b[None, :]),
        p2=p2, p2b=p2b,
        glanw=jnp.tile(gla_norm_w, GLA_HEADS)[None, :], dskip=jnp.repeat(d_skip, SSM_HEAD_DIM)[None, :],
        ssmnw=ssm_norm_w[None, :], wbg=w_br_gla.astype(BF16), wbs=w_br_ssm.astype(BF16),
        wo=w_out.astype(BF16), fnw=norm_ffn_w[None, :], wg=w_ffn_gate.astype(BF16),
        wu=w_ffn_up.astype(BF16), wd=w_ffn_down.astype(BF16), finw=norm_final_w[None, :])


def _run(x, p):
    b, l, _ = x.shape
    n = b * l
    t = _tile(l, 512)
    x2d = x.reshape(n, D_MODEL)
    qk, v, sog, sz, xa, sgg, sgs, lg, la, ldt = _inproj(x2d, l, p["nw"], p["proj_ws"], p["proj_consts"], t)
    r3 = lambda a: a.reshape(b, l, a.shape[-1])
    sgb, ssb = _bwd_states(r3(qk), r3(v), r3(xa), r3(lg), r3(la), r3(ldt), p["p2b"], t)
    sgb = sgb.reshape(n // t, GLA_DV, GLA_QK)
    ssb = ssb.reshape(n // t, SSM_BC, 256)
    consts = (p["glanw"], p["dskip"], p["ssmnw"], p["p2"], p["wbg"], p["wbs"], p["wo"])
    xm = _mixer(x2d, qk, v, sog, sz, xa, sgg, sgs, lg, la, ldt, sgb, ssb, consts, l, t)
    y = _ffn(xm, p["fnw"], p["wg"], p["wu"], p["wd"], p["finw"], _tile(n, 512))
    return y.reshape(b, l, D_MODEL)


def kernel(x_prompt, x_sample, norm_mix_w, w_in, gla_up_f, gla_bias_f, gla_up_b, gla_bias_b, gla_norm_w, conv_w,
           conv_b, dt_bias_f, dt_bias_b, a_log_f, a_log_b, d_skip, ssm_norm_w, w_br_gla, w_br_ssm, w_out,
           norm_ffn_w, w_ffn_gate, w_ffn_up, w_ffn_down, norm_final_w):
    assert norm_mix_w.shape[0] == 1, "single-layer block"
    p = _prep_weights(norm_mix_w[0], w_in[0], gla_up_f[0], gla_bias_f[0], gla_up_b[0], gla_bias_b[0],
                      gla_norm_w[0], conv_w[0], conv_b[0], dt_bias_f[0], dt_bias_b[0], a_log_f[0], a_log_b[0],
                      d_skip[0], ssm_norm_w[0], w_br_gla[0], w_br_ssm[0], w_out[0], norm_ffn_w[0],
                      w_ffn_gate[0], w_ffn_up[0], w_ffn_down[0], norm_final_w)
    return (_run(x_prompt, p), _run(x_sample, p))
```

```python
import functools
import math

import jax
import jax.numpy as jnp
from jax import lax
from jax.experimental import pallas as pl
from jax.experimental.pallas import tpu as pltpu

F32 = jnp.float32
BF16 = jnp.bfloat16

D_MODEL = 1024
GLA_HEADS = 4
GLA_DK = 64
GLA_DV = 128
GLA_QK = GLA_HEADS * GLA_DK
GLA_V = GLA_HEADS * GLA_DV
GLA_GATE_RANK = 16
GLA_GATE_NORM = 16.0
SSM_HEAD_DIM = 64
SSM_INNER = 1024
SSM_HEADS = 16
SSM_GROUPS = 4
SSM_HPG = 4
SSM_STATE = 64
SSM_BC = SSM_GROUPS * SSM_STATE
SSM_CONV = 5
SSM_XBC = SSM_INNER + 2 * SSM_BC
D_FF = 2816
EPS = 1e-6
LOG2E = math.log2(math.e)
IN_SIZES = (GLA_QK, GLA_QK, GLA_V, GLA_V, GLA_GATE_RANK, GLA_GATE_RANK,
            SSM_INNER, SSM_XBC, SSM_HEADS, SSM_HEADS, D_MODEL, D_MODEL)

SMALL_W = 128
LANE_RF = 0
LANE_RB = 16
LANE_DTF = 32
LANE_DTB = 48

LANES = 128
CHUNK = 128
HALO = 8
VMEM_LIMIT = 56 * 1024 * 1024


def _dot(a, b):
    return jnp.dot(a, b, preferred_element_type=F32)


def _dot_nt(a, b):
    return lax.dot_general(a, b, (((1,), (1,)), ((), ())), preferred_element_type=F32)


def _dot_tn(a, b):
    return lax.dot_general(a, b, (((0,), (0,)), ((), ())), preferred_element_type=F32)


def _split2(x):
    hi = x.astype(BF16)
    return hi, (x - hi.astype(F32)).astype(BF16)


def _dot_exact_lhs(m, x):
    hi, lo = _split2(x)
    return _dot(m, hi) + _dot(m, lo)


def _rms(x, w):
    return x * lax.rsqrt(jnp.mean(x * x, axis=-1, keepdims=True) + EPS) * w


def _sigmoid_of_half(h):
    return 0.5 + 0.5 * jnp.tanh(h)


def _silu_of_half(h):
    return h + h * jnp.tanh(h)


def _silu(x):
    return _silu_of_half(0.5 * x)


def _softplus(x):
    return jnp.maximum(x, 0.0) + jnp.log1p(jnp.exp(-jnp.abs(x)))


def _tri_mask(n):
    row = lax.broadcasted_iota(jnp.int32, (n, n), 0)
    col = lax.broadcasted_iota(jnp.int32, (n, n), 1)
    return col <= row


def _lane_block(width, block):
    return lax.broadcasted_iota(jnp.int32, (1, width), 1) // block


def _stack_row(row):
    return jnp.concatenate(
        [jnp.broadcast_to(row[:, g * 256:(g + 1) * 256], (SSM_STATE, 256)) for g in range(SSM_GROUPS)], axis=0)


def _inproj_kernel(tiles_per_row, x_ref, xp_ref, xn_ref, nw_ref,
                   wqk_ref, wv_ref, wog_ref, wz_ref, wxbc_ref, wgg_ref, wgs_ref, wsm_ref,
                   upc_ref, gbias_ref, dtbias_ref, alog_ref, cwh_ref, cbh_ref,
                   qk_ref, v_ref, sog_ref, sz_ref, xa_ref, sgg_ref, sgs_ref, lg_ref, la_ref, ldt_ref,
                   xp_s):
    i = pl.program_id(0)
    tm = x_ref.shape[0]
    nw = nw_ref[...]
    u = _rms(x_ref[...], nw).astype(BF16)

    pos = i % tiles_per_row
    has_prev = jnp.where(pos > 0, 1.0, 0.0)
    has_next = jnp.where(pos < tiles_per_row - 1, 1.0, 0.0)
    xh = jnp.concatenate([xp_ref[...], xn_ref[...]], axis=0)
    xbc_h = _dot(_rms(xh, nw).astype(BF16), wxbc_ref[...])
    xbc = _dot(u, wxbc_ref[...])
    n_slab = SSM_XBC // LANES
    for j in range(n_slab):
        cols = slice(j * LANES, (j + 1) * LANES)
        xp_s[j, HALO:HALO + tm, :] = xbc[:, cols]
        xp_s[j, 0:HALO, :] = xbc_h[0:HALO, cols] * has_prev
        xp_s[j, HALO + tm:2 * HALO + tm, :] = xbc_h[HALO:2 * HALO, cols] * has_next

    def conv_slab(j):
        cols = slice(j * LANES, (j + 1) * LANES)
        acc = cbh_ref[:, cols] + cwh_ref[0:1, cols] * xp_s[j, HALO - SSM_CONV // 2:HALO - SSM_CONV // 2 + tm, :]
        for tap in range(1, SSM_CONV):
            off = HALO + tap - SSM_CONV // 2
            acc = acc + cwh_ref[tap:tap + 1, cols] * xp_s[j, off:off + tm, :]
        xa_ref[:, cols] = _silu_of_half(acc).astype(BF16)

    sm = _dot(u, wsm_ref[...])
    xg = _dot(sm.astype(BF16), upc_ref[...]) + gbias_ref[...]
    lg_ref[...] = (jnp.minimum(xg, 0.0) * (LOG2E / GLA_GATE_NORM)
                   - jnp.log2(1.0 + jnp.exp2(jnp.abs(xg) * (-LOG2E))) * (1.0 / GLA_GATE_NORM))
    dt = _softplus(sm + dtbias_ref[...])
    lane = lax.broadcasted_iota(jnp.int32, (1, SMALL_W), 1)
    live = (lane >= LANE_DTF) & (lane < LANE_DTB + SSM_HEADS)
    la_ref[...] = dt * jnp.where(live, -LOG2E * jnp.exp(alog_ref[...]), 0.0)
    ldt_ref[...] = jnp.log2(dt)

    groups = [
        lambda: qk_ref.__setitem__(..., _dot(u, wqk_ref[...]).astype(BF16)),
        lambda: v_ref.__setitem__(..., _dot(u, wv_ref[...]).astype(BF16)),
        lambda: sog_ref.__setitem__(..., _silu_of_half(_dot(u, wog_ref[...])).astype(BF16)),
        lambda: sz_ref.__setitem__(..., _silu_of_half(_dot(u, wz_ref[...])).astype(BF16)),
        lambda: sgg_ref.__setitem__(..., _sigmoid_of_half(_dot(u, wgg_ref[...])).astype(BF16)),
        lambda: sgs_ref.__setitem__(..., _sigmoid_of_half(_dot(u, wgs_ref[...])).astype(BF16)),
    ]
    per = n_slab // len(groups)
    for gi, emit in enumerate(groups):
        for j in range(gi * per, (gi + 1) * per):
            conv_slab(j)
        emit()


def _const_spec(shape):
    nd = len(shape)
    return pl.BlockSpec(shape, lambda *_: (0,) * nd, pipeline_mode=pl.Buffered(1))


def _inproj(x2d, seq_len, nw, ws, consts, tm):
    n = x2d.shape[0]
    hb = tm // HALO
    nh = n // HALO
    widths = [2 * GLA_QK, GLA_V, GLA_V, SSM_INNER, SSM_XBC, D_MODEL, D_MODEL, 2 * GLA_QK, SMALL_W, SMALL_W]
    dtypes = [BF16] * 7 + [F32] * 3
    row = lambda i: (i, 0)
    return pl.pallas_call(
        functools.partial(_inproj_kernel, seq_len // tm),
        grid=(n // tm,),
        in_specs=[pl.BlockSpec((tm, D_MODEL), row),
                  pl.BlockSpec((HALO, D_MODEL), lambda i: (jnp.maximum(i * hb - 1, 0), 0)),
                  pl.BlockSpec((HALO, D_MODEL), lambda i: (jnp.minimum((i + 1) * hb, nh - 1), 0)),
                  _const_spec(nw.shape)]
                 + [_const_spec(w.shape) for w in ws] + [_const_spec(a.shape) for a in consts],
        out_specs=[pl.BlockSpec((tm, w), row) for w in widths],
        out_shape=[jax.ShapeDtypeStruct((n, w), dt) for w, dt in zip(widths, dtypes)],
        scratch_shapes=[pltpu.VMEM((SSM_XBC // LANES, tm + 2 * HALO, LANES), F32)],
        compiler_params=pltpu.CompilerParams(dimension_semantics=("arbitrary",), vmem_limit_bytes=VMEM_LIMIT),
        name="inproj",
    )(x2d, x2d, x2d, nw, *ws, *consts)


def _bwd_state_kernel(k_ref, v_ref, xa_ref, lgb_ref, la_ref, ldt_ref, p2b_ref,
                      sgb_ref, ssb_ref,
                      sg_s, ss_s):
    s = pl.program_id(1)
    t = k_ref.shape[0]
    nc = t // CHUNK

    @pl.when(s == 0)
    def _():
        sg_s[...] = jnp.zeros_like(sg_s)
        ss_s[...] = jnp.zeros_like(ss_s)

    sgb_ref[...] = sg_s[...]
    ssb_ref[...] = ss_s[...]

    zz = jnp.concatenate([lgb_ref[...], la_ref[...]], axis=1)
    lmat = jnp.where(_tri_mask(CHUNK), 1.0, 0.0).astype(BF16)
    carry = jnp.zeros((1, GLA_QK + SMALL_W), F32)
    parts = []
    for c in range(nc):
        cs = _dot_exact_lhs(lmat, zz[c * CHUNK:(c + 1) * CHUNK]) + carry
        parts.append(cs)
        carry = cs[CHUNK - 1:CHUNK, :]
    ics = jnp.concatenate(parts, axis=0)
    tot = carry
    ecs = ics - zz

    kbs = (k_ref[...].astype(F32) * jnp.exp2(ecs[:, :GLA_QK])).astype(BF16)
    head_of_lane = _lane_block(GLA_QK, GLA_DK)
    new_g = sg_s[...] * jnp.exp2(tot[:, :GLA_QK])
    vt_k = _dot_tn(v_ref[...], kbs)
    for h in range(GLA_HEADS):
        new_g = new_g + jnp.where(head_of_lane == h, vt_k[h * GLA_DV:(h + 1) * GLA_DV, :], 0.0)
    sg_s[...] = new_g

    p2b = p2b_ref[...]
    wb = jnp.exp2(ecs[:, GLA_QK:] + ldt_ref[...])
    et_hi, et_lo = _split2(jnp.broadcast_to(jnp.exp2(tot[:, GLA_QK:]), (8, SMALL_W)))
    exp_all = _dot(jnp.concatenate([wb.astype(BF16), et_hi, et_lo], axis=0), p2b)
    wexp = exp_all[0:t, :]
    dec = _stack_row(exp_all[t:t + 1, :] + exp_all[t + 8:t + 9, :])
    xw = (xa_ref[:, 0:SSM_INNER].astype(F32) * wexp).astype(BF16)
    contrib = jnp.concatenate(
        [_dot_tn(xa_ref[:, SSM_INNER + g * SSM_STATE:SSM_INNER + (g + 1) * SSM_STATE], xw[:, g * 256:(g + 1) * 256])
         for g in range(SSM_GROUPS)], axis=0)
    ss_s[...] = ss_s[...] * dec + contrib


def _bwd_states(qk, v, xa, lg, la, ldt, p2b, t):
    b, l, _ = qk.shape
    ns = l // t
    rev = lambda i, s: (i, ns - 1 - s, 0)
    rev1 = lambda i, s: (i, ns - 1 - s, 1)
    return pl.pallas_call(
        _bwd_state_kernel,
        grid=(b, ns),
        in_specs=[
            pl.BlockSpec((None, t, GLA_QK), rev1),
            pl.BlockSpec((None, t, GLA_V), rev),
            pl.BlockSpec((None, t, SSM_INNER + SSM_BC), rev),
            pl.BlockSpec((None, t, GLA_QK), rev1),
            pl.BlockSpec((None, t, SMALL_W), rev),
            pl.BlockSpec((None, t, SMALL_W), rev),
            _const_spec(p2b.shape),
        ],
        out_specs=[
            pl.BlockSpec((None, None, GLA_DV, GLA_QK), lambda i, s: (i, ns - 1 - s, 0, 0)),
            pl.BlockSpec((None, None, SSM_BC, 256), lambda i, s: (i, ns - 1 - s, 0, 0)),
        ],
        out_shape=[
            jax.ShapeDtypeStruct((b, ns, GLA_DV, GLA_QK), F32),
            jax.ShapeDtypeStruct((b, ns, SSM_BC, 256), F32),
        ],
        scratch_shapes=[pltpu.VMEM((GLA_DV, GLA_QK), F32), pltpu.VMEM((SSM_BC, 256), F32)],
        compiler_params=pltpu.CompilerParams(dimension_semantics=("arbitrary", "arbitrary"),
                                             vmem_limit_bytes=VMEM_LIMIT),
        name="bwd_states",
    )(qk, v, xa, lg, la, ldt, p2b)


def _mixer_kernel(x_ref, qk_ref, v_ref, sog_ref, sz_ref, xa_ref, sgg_ref, sgs_ref, lg_ref, la_ref, ldt_ref,
                  sgb_ref, ssb_ref,
                  glanw_ref, dskip_ref, ssmnw_ref, p2_ref, wbg_ref, wbs_ref, wo_ref,
                  out_ref,
                  sgf_s, ssf_s, af_s, rb_s, qf_s, qb_s, kf_s, kb_s, qfi_s, qbi_s, etf_s, etb_s,
                  kvf_s, kvb_s, stf_s, stb_s, dsf_s, dsb_s, decf_s, decb_s, ssf_c, ssb_c, o_s, *, ns):
    g = pl.program_id(0)
    t = x_ref.shape[0]
    nc = t // CHUNK
    C = CHUNK

    @pl.when(g % ns == 0)
    def _():
        sgf_s[...] = jnp.zeros_like(sgf_s)
        ssf_s[...] = jnp.zeros_like(ssf_s)

    @pl.when(g == 0)
    def _():
        o_s[...] = jnp.zeros_like(o_s)

    slot_w = g % 2
    slot_r = 1 - slot_w

    def finish(part, nparts):
        rr = slice(part * (t // nparts), (part + 1) * (t // nparts))
        parts = []
        for h in range(GLA_HEADS):
            oh = o_s[slot_r, rr, h * GLA_DV:(h + 1) * GLA_DV]
            parts.append(oh * lax.rsqrt(jnp.mean(oh * oh, axis=-1, keepdims=True) + EPS))
        gl = (jnp.concatenate(parts, axis=1) * glanw_ref[...] * sog_ref[rr, :].astype(F32)).astype(BF16)
        y = _rms(o_s[slot_r, rr, GLA_V:] * sz_ref[rr, :].astype(F32), ssmnw_ref[...]).astype(BF16)
        m = (sgg_ref[rr, :].astype(F32) * _dot(gl, wbg_ref[...])
             + sgs_ref[rr, :].astype(F32) * _dot(y, wbs_ref[...]))
        out_ref[rr, :] = x_ref[rr, :] + _dot(m.astype(BF16), wo_ref[...])

    tri = _tri_mask(C)
    lmat = jnp.where(tri, 1.0, 0.0).astype(BF16)
    head_of_lane = _lane_block(GLA_QK, GLA_DK)
    group_of_lane = _lane_block(SSM_BC, SSM_STATE)
    lane128 = lax.broadcasted_iota(jnp.int32, (1, SMALL_W), 1)
    fwd_lane = lane128 < LANE_DTB
    lane_lo = lane128 < SSM_HEAD_DIM
    p2 = p2_ref[...]
    zero_bf = jnp.zeros((), BF16)

    def chunk_a(c):
        rows = slice(c * C, (c + 1) * C)
        la = la_ref[rows, :]
        lg = lg_ref[rows, :]
        cs = _dot_exact_lhs(lmat, jnp.concatenate([lg, la], axis=1))
        ics, ics_b, af = cs[:, :GLA_QK], cs[:, GLA_QK:2 * GLA_QK], cs[:, 2 * GLA_QK:]
        rcs = ics_b[C - 1:C, :] - ics_b + lg[:, GLA_QK:]
        rb = af[C - 1:C, :] - af + la
        af_s[rows, :] = af
        rb_s[rows, :] = rb

        m_f = ics[C // 2 - 1:C // 2, :]
        tot_f = ics[C - 1:C, :]
        m_b = rcs[C // 2:C // 2 + 1, :]
        tot_b = rcs[0:1, :]
        q = qk_ref[rows, 0:GLA_QK].astype(F32)
        k = qk_ref[rows, GLA_QK:2 * GLA_QK].astype(F32)
        qf32 = q * jnp.exp2(ics - m_f)
        kf32 = k * jnp.exp2(m_f - ics)
        qb32 = q * jnp.exp2(rcs - m_b)
        kb32 = k * jnp.exp2(m_b - rcs)
        kfs = (kf32 * jnp.exp2(tot_f - m_f)).astype(BF16)
        kbs = (kb32 * jnp.exp2(tot_b - m_b)).astype(BF16)
        qf_s[rows, :] = qf32.astype(BF16)
        qb_s[rows, :] = qb32.astype(BF16)
        kf_s[rows, :] = kf32.astype(BF16)
        kb_s[rows, :] = kb32.astype(BF16)
        qfi_s[rows, :] = (qf32 * jnp.exp2(m_f)).astype(BF16)
        qbi_s[rows, :] = (qb32 * jnp.exp2(m_b)).astype(BF16)
        etf_s[c] = jnp.broadcast_to(jnp.exp2(tot_f), (8, GLA_QK))
        etb_s[c] = jnp.broadcast_to(jnp.exp2(tot_b), (8, GLA_QK))
        vt_kf = _dot_tn(v_ref[rows, :], kfs)
        vt_kb = _dot_tn(v_ref[rows, :], kbs)
        kvf = jnp.zeros((GLA_DV, GLA_QK), F32)
        kvb = jnp.zeros((GLA_DV, GLA_QK), F32)
        for h in range(GLA_HEADS):
            hm = head_of_lane == h
            kvf = kvf + jnp.where(hm, vt_kf[h * GLA_DV:(h + 1) * GLA_DV, :], 0.0)
            kvb = kvb + jnp.where(hm, vt_kb[h * GLA_DV:(h + 1) * GLA_DV, :], 0.0)
        kvf_s[c] = kvf
        kvb_s[c] = kvb

        tot_a = af[C - 1:C, :]
        tot_r = rb[0:1, :]
        sw = jnp.exp2(jnp.where(fwd_lane, tot_a - af, tot_r - rb) + ldt_ref[rows, :])
        return sw.astype(BF16), jnp.exp2(jnp.where(fwd_lane, tot_a, tot_r))

    def chunk_a2(c, swx):
        rows = slice(c * C, (c + 1) * C)
        xsf = xa_ref[rows, 0:SSM_INNER].astype(F32)
        bm = xa_ref[rows, SSM_INNER:SSM_INNER + SSM_BC]
        xwf = (xsf * swx[:, :SSM_INNER].astype(F32)).astype(BF16)
        xwb = (xsf * swx[:, SSM_INNER:].astype(F32)).astype(BF16)
        bms = [bm[:, g * SSM_STATE:(g + 1) * SSM_STATE] for g in range(SSM_GROUPS)]
        dsf_s[c] = jnp.concatenate(
            [_dot_tn(bms[g], xwf[:, g * 256:(g + 1) * 256]) for g in range(SSM_GROUPS)], axis=0)
        dsb_s[c] = jnp.concatenate(
            [_dot_tn(bms[g], xwb[:, g * 256:(g + 1) * 256]) for g in range(SSM_GROUPS)], axis=0)

    n_fin = 2 if nc % 2 == 0 else 1
    sws, ets = [], []
    for c in range(nc):
        sw_c, et_c = chunk_a(c)
        sws.append(sw_c)
        ets.append(et_c)
    et_hi, et_lo = _split2(jnp.concatenate(ets + [jnp.zeros((8 - nc, SMALL_W), F32)], axis=0))
    exp_all = _dot(jnp.concatenate(sws + [et_hi, et_lo], axis=0), p2)
    swx_all = exp_all[0:t, :].astype(BF16)
    etx = exp_all[t:t + 8, :] + exp_all[t + 8:t + 16, :]
    for c in range(nc):
        chunk_a2(c, swx_all[c * C:(c + 1) * C, :])
        decf_s[c] = _stack_row(etx[c:c + 1, :SSM_INNER])
        decb_s[c] = _stack_row(etx[c:c + 1, SSM_INNER:])
        if n_fin == 2 and c == nc - 1:
            finish(0, 2)

    sg = sgf_s[...]
    ss = ssf_s[...]
    for c in range(nc):
        stf_s[c] = sg.astype(BF16)
        ssf_c[c] = ss.astype(BF16)
        sg = sg * etf_s[c][0:1, :] + kvf_s[c]
        ss = ss * decf_s[c] + dsf_s[c]
    sgf_s[...] = sg
    ssf_s[...] = ss
    sg = sgb_ref[...]
    ss = ssb_ref[...]
    for c in reversed(range(nc)):
        stb_s[c] = sg.astype(BF16)
        ssb_c[c] = ss.astype(BF16)
        sg = sg * etb_s[c][0:1, :] + kvb_s[c]
        ss = ss * decb_s[c] + dsb_s[c]

    tri4 = jnp.concatenate([tri] * 4, axis=1)

    def chunk_o(c):
        rows = slice(c * C, (c + 1) * C)
        kf = kf_s[rows, :]
        kb = kb_s[rows, :]
        kf_m = jnp.concatenate([jnp.where(head_of_lane == h, kf, zero_bf) for h in range(GLA_HEADS)], axis=0)
        kb_m = jnp.concatenate([jnp.where(head_of_lane == h, kb, zero_bf) for h in range(GLA_HEADS)], axis=0)
        att = jnp.where(tri4, _dot_nt(qf_s[rows, :], kf_m), _dot_nt(qb_s[rows, :], kb_m)).astype(BF16)
        stf = stf_s[c]
        stb = stb_s[c]
        for h in range(GLA_HEADS):
            blk = slice((h // 2) * LANES, (h // 2 + 1) * LANES)
            hm = (lane128 // GLA_DK) == (h % 2)
            qi = jnp.concatenate([jnp.where(hm, qfi_s[rows, blk], zero_bf),
                                  jnp.where(hm, qbi_s[rows, blk], zero_bf)], axis=1)
            st = jnp.concatenate([stf[:, blk], stb[:, blk]], axis=1)
            vh = v_ref[rows, h * GLA_DV:(h + 1) * GLA_DV]
            o_s[slot_w, rows, h * GLA_DV:(h + 1) * GLA_DV] = _dot(att[:, h * C:(h + 1) * C], vh) + _dot_nt(qi, st)

        xs = xa_ref[rows, 0:SSM_INNER]
        bm = xa_ref[rows, SSM_INNER:SSM_INNER + SSM_BC]
        cm = xa_ref[rows, SSM_INNER + SSM_BC:SSM_XBC]
        af = af_s[rows, :]
        rb = rb_s[rows, :]
        ldt_t = ldt_ref[rows, :].T
        arow_t = af.T - ldt_t
        rrow_t = rb.T - ldt_t
        bm_m = jnp.concatenate([jnp.where(group_of_lane == g, bm, zero_bf) for g in range(SSM_GROUPS)], axis=0)
        cbm_all = _dot_nt(cm, bm_m)
        cm32 = cm.astype(F32)
        ssf = ssf_c[c]
        ssb = ssb_c[c]
        for g in range(SSM_GROUPS):
            cbm = cbm_all[:, g * C:(g + 1) * C]
            cblk = cm32[:, (g // 2) * LANES:(g // 2 + 1) * LANES]
            crot = pltpu.roll(cblk, SSM_STATE, 1)
            cm2 = jnp.where(lane_lo, cblk, crot) if g % 2 == 0 else jnp.where(lane_lo, crot, cblk)
            srow = slice(g * SSM_STATE, (g + 1) * SSM_STATE)
            for pair in range(SSM_HPG // 2):
                blk = (g * SSM_HPG + pair * 2) * SSM_HEAD_DIM
                pcol = slice(pair * LANES, (pair + 1) * LANES)
                rhs = jnp.concatenate([xs[:, blk:blk + LANES], ssf[srow, pcol], ssb[srow, pcol]], axis=0)
                lhs = []
                for jj in range(2):
                    hd = g * SSM_HPG + pair * 2 + jj
                    lf = LANE_DTF + hd
                    lb = LANE_DTB + hd
                    a_col = jnp.broadcast_to(af[:, lf:lf + 1], (C, C))
                    r_col = jnp.broadcast_to(rb[:, lb:lb + 1], (C, C))
                    diff = jnp.where(tri, a_col - arow_t[lf:lf + 1, :], r_col - rrow_t[lb:lb + 1, :])
                    mm = (cbm * jnp.exp2(diff)).astype(BF16)
                    ecol = jnp.exp2(jnp.where(lane_lo, a_col, r_col))
                    lhs.append(jnp.concatenate([mm, (cm2 * ecol).astype(BF16)], axis=1))
                yy = _dot(jnp.concatenate(lhs, axis=0), rhs)
                xpair = xs[:, blk:blk + LANES].astype(F32)
                o_s[slot_w, rows, GLA_V + blk:GLA_V + blk + LANES] = (
                    jnp.where(lane_lo, yy[0:C, :], yy[C:2 * C, :]) + dskip_ref[:, blk:blk + LANES] * xpair)

    for c in range(nc):
        chunk_o(c)
        if c == nc // 2 - 1 or nc == 1:
            finish(n_fin - 1, n_fin)


def _mixer(x, qk, v, sog, sz, xa, sgg, sgs, lg, la, ldt, sgb, ssb, consts, seq_len, t):
    n = x.shape[0]
    ns = seq_len // t
    nsteps = n // t
    nc = t // CHUNK
    cur = lambda g: (jnp.minimum(g, nsteps - 1), 0)
    lag = lambda g: (jnp.maximum(g - 1, 0), 0)
    tok = lambda w: pl.BlockSpec((t, w), cur)
    tok_lag = lambda w: pl.BlockSpec((t, w), lag)
    st = lambda r, c: pl.BlockSpec((None, r, c), lambda g: (jnp.minimum(g, nsteps - 1), 0, 0))
    scratch = [
        pltpu.VMEM((GLA_DV, GLA_QK), F32),
        pltpu.VMEM((SSM_BC, 256), F32),
        pltpu.VMEM((t, SMALL_W), F32),
        pltpu.VMEM((t, SMALL_W), F32),
        pltpu.VMEM((t, GLA_QK), BF16),
        pltpu.VMEM((t, GLA_QK), BF16),
        pltpu.VMEM((t, GLA_QK), BF16),
        pltpu.VMEM((t, GLA_QK), BF16),
        pltpu.VMEM((t, GLA_QK), BF16),
        pltpu.VMEM((t, GLA_QK), BF16),
        pltpu.VMEM((nc, 8, GLA_QK), F32),
        pltpu.VMEM((nc, 8, GLA_QK), F32),
        pltpu.VMEM((nc, GLA_DV, GLA_QK), F32),
        pltpu.VMEM((nc, GLA_DV, GLA_QK), F32),
        pltpu.VMEM((nc, GLA_DV, GLA_QK), BF16),
        pltpu.VMEM((nc, GLA_DV, GLA_QK), BF16),
        pltpu.VMEM((nc, SSM_BC, 256), F32),
        pltpu.VMEM((nc, SSM_BC, 256), F32),
        pltpu.VMEM((nc, SSM_BC, 256), F32),
        pltpu.VMEM((nc, SSM_BC, 256), F32),
        pltpu.VMEM((nc, SSM_BC, 256), BF16),
        pltpu.VMEM((nc, SSM_BC, 256), BF16),
        pltpu.VMEM((2, t, GLA_V + SSM_INNER), F32),
    ]
    return pl.pallas_call(
        functools.partial(_mixer_kernel, ns=ns),
        grid=(nsteps + 1,),
        in_specs=[tok_lag(D_MODEL), tok(2 * GLA_QK), tok(GLA_V), tok_lag(GLA_V), tok_lag(SSM_INNER), tok(SSM_XBC),
                  tok_lag(D_MODEL), tok_lag(D_MODEL), tok(2 * GLA_QK), tok(SMALL_W), tok(SMALL_W),
                  st(GLA_DV, GLA_QK), st(SSM_BC, 256)]
                 + [_const_spec(a.shape) for a in consts],
        out_specs=tok_lag(D_MODEL),
        out_shape=jax.ShapeDtypeStruct((n, D_MODEL), F32),
        scratch_shapes=scratch,
        compiler_params=pltpu.CompilerParams(dimension_semantics=("arbitrary",), vmem_limit_bytes=VMEM_LIMIT),
        name="mixer",
    )(x, qk, v, sog, sz, xa, sgg, sgs, lg, la, ldt, sgb, ssb, *consts)


FFN_COL_CHUNKS = ((0, 768), (768, 1536), (1536, 2304), (2304, D_FF))


def _ffn_kernel(x_ref, nw_ref, wg_ref, wu_ref, wd_ref, fw_ref, out_ref):
    x = x_ref[...]
    h = _rms(x, nw_ref[...]).astype(BF16)
    y = x
    for lo, hi in FFN_COL_CHUNKS:
        a = (_silu(_dot(h, wg_ref[:, lo:hi])) * _dot(h, wu_ref[:, lo:hi])).astype(BF16)
        y = y + _dot(a, wd_ref[lo:hi, :])
    out_ref[...] = _rms(y, fw_ref[...])


def _ffn(x2d, nw, wg, wu, wd, fw, tm):
    n = x2d.shape[0]
    row = lambda i: (i, 0)
    return pl.pallas_call(
        _ffn_kernel,
        grid=(n // tm,),
        in_specs=[pl.BlockSpec((tm, D_MODEL), row)] + [_const_spec(a.shape) for a in (nw, wg, wu, wd, fw)],
        out_specs=pl.BlockSpec((tm, D_MODEL), row),
        out_shape=jax.ShapeDtypeStruct((n, D_MODEL), F32),
        compiler_params=pltpu.CompilerParams(dimension_semantics=("arbitrary",), vmem_limit_bytes=VMEM_LIMIT),
        name="ffn",
    )(x2d, nw, wg, wu, wd, fw)


def _tile(n, pref):
    t = pref
    while n % t:
        t //= 2
    return t


def _prep_weights(norm_mix_w, w_in, gla_up_f, gla_bias_f, gla_up_b, gla_bias_b, gla_norm_w, conv_w, conv_b,
                  dt_bias_f, dt_bias_b, a_log_f, a_log_b, d_skip, ssm_norm_w, w_br_gla, w_br_ssm, w_out,
                  norm_ffn_w, w_ffn_gate, w_ffn_up, w_ffn_down, norm_final_w):
    offs = [0]
    for sz in IN_SIZES:
        offs.append(offs[-1] + sz)
    w_in = w_in.astype(BF16)
    col = lambda i: w_in[:, offs[i]:offs[i + 1]]
    wq, wk, wv, wog, wrf, wrb, wz, wxbc, wdtf, wdtb, wgg, wgs = [col(i) for i in range(12)]
    wqk = jnp.concatenate([wq * (GLA_DK ** -0.5), wk], axis=1)
    wsm = jnp.concatenate([wrf, wrb, wdtf, wdtb, jnp.zeros((D_MODEL, SMALL_W - 64), BF16)], axis=1)
    proj_ws = [w.astype(BF16) for w in (wqk, wv, 0.5 * wog, 0.5 * wz, wxbc, 0.5 * wgg, 0.5 * wgs, wsm)]

    upc = jnp.zeros((SMALL_W, 2 * GLA_QK), F32)
    upc = upc.at[LANE_RF:LANE_RF + GLA_GATE_RANK, :GLA_QK].set(gla_up_f)
    upc = upc.at[LANE_RB:LANE_RB + GLA_GATE_RANK, GLA_QK:].set(gla_up_b)
    upc = upc.astype(BF16)
    gbias = jnp.concatenate([gla_bias_f, gla_bias_b])[None, :]

    def small_row(f, b):
        r = jnp.zeros((1, SMALL_W), F32)
        r = r.at[0, LANE_DTF:LANE_DTF + SSM_HEADS].set(f)
        return r.at[0, LANE_DTB:LANE_DTB + SSM_HEADS].set(b)

    dtbias = small_row(dt_bias_f, dt_bias_b)
    alog = small_row(a_log_f, a_log_b)

    lane = jnp.arange(SMALL_W)[:, None]
    blk = jnp.arange(SSM_INNER)[None, :] // SSM_HEAD_DIM
    p2f = (lane == LANE_DTF + blk)
    p2b = (lane == LANE_DTB + blk)
    p2 = jnp.concatenate([p2f, p2b], axis=1).astype(BF16)
    p2b = p2b.astype(BF16)

    cwh = 0.5 * jnp.concatenate([conv_w, jnp.zeros((8 - SSM_CONV, SSM_XBC), F32)], axis=0)
    return dict(
        nw=norm_mix_w[None, :], proj_ws=proj_ws,
        proj_consts=(upc, gbias, dtbias, alog, cwh, 0.5 * conv_b[None, :]),
        p2=p2, p2b=p2b,
        glanw=jnp.tile(gla_norm_w, GLA_HEADS)[None, :], dskip=jnp.repeat(d_skip, SSM_HEAD_DIM)[None, :],
        ssmnw=ssm_norm_w[None, :], wbg=w_br_gla.astype(BF16), wbs=w_br_ssm.astype(BF16),
        wo=w_out.astype(BF16), fnw=norm_ffn_w[None, :], wg=w_ffn_gate.astype(BF16),
        wu=w_ffn_up.astype(BF16), wd=w_ffn_down.astype(BF16), finw=norm_final_w[None, :])


def _run(x, p):
    b, l, _ = x.shape
    n = b * l
    t = _tile(l, 512)
    x2d = x.reshape(n, D_MODEL)
    qk, v, sog, sz, xa, sgg, sgs, lg, la, ldt = _inproj(x2d, l, p["nw"], p["proj_ws"], p["proj_consts"], t)
    r3 = lambda a: a.reshape(b, l, a.shape[-1])
    sgb, ssb = _bwd_states(r3(qk), r3(v), r3(xa), r3(lg), r3(la), r3(ldt), p["p2b"], t)
    sgb = sgb.reshape(n // t, GLA_DV, GLA_QK)
    ssb = ssb.reshape(n // t, SSM_BC, 256)
    consts = (p["glanw"], p["dskip"], p["ssmnw"], p["p2"], p["wbg"], p["wbs"], p["wo"])
    xm = _mixer(x2d, qk, v, sog, sz, xa, sgg, sgs, lg, la, ldt, sgb, ssb, consts, l, t)
    y = _ffn(xm, p["fnw"], p["wg"], p["wu"], p["wd"], p["finw"], _tile(n, 1024))
    return y.reshape(b, l, D_MODEL)


def kernel(x_prompt, x_sample, norm_mix_w, w_in, gla_up_f, gla_bias_f, gla_up_b, gla_bias_b, gla_norm_w, conv_w,
           conv_b, dt_bias_f, dt_bias_b, a_log_f, a_log_b, d_skip, ssm_norm_w, w_br_gla, w_br_ssm, w_out,
           norm_ffn_w, w_ffn_gate, w_ffn_up, w_ffn_down, norm_final_w):
    assert norm_mix_w.shape[0] == 1, "single-layer block"
    p = _prep_weights(norm_mix_w[0], w_in.reshape(w_in.shape[1:]), gla_up_f[0], gla_bias_f[0], gla_up_b[0], gla_bias_b[0],
                      gla_norm_w[0], conv_w[0], conv_b[0], dt_bias_f[0], dt_bias_b[0], a_log_f[0], a_log_b[0],
                      d_skip[0], ssm_norm_w[0], w_br_gla[0], w_br_ssm[0], w_out[0], norm_ffn_w[0],
                      w_ffn_gate[0], w_ffn_up[0], w_ffn_down[0], norm_final_w)
    return (_run(x_prompt, p), _run(x_sample, p))
```

```python
import functools
import math

import jax
import jax.numpy as jnp
from jax import lax
from jax.experimental import pallas as pl
from jax.experimental.pallas import tpu as pltpu

F32 = jnp.float32
BF16 = jnp.bfloat16

D_MODEL = 1024
GLA_HEADS = 4
GLA_DK = 64
GLA_DV = 128
GLA_QK = GLA_HEADS * GLA_DK
GLA_V = GLA_HEADS * GLA_DV
GLA_GATE_RANK = 16
GLA_GATE_NORM = 16.0
SSM_HEAD_DIM = 64
SSM_INNER = 1024
SSM_HEADS = 16
SSM_GROUPS = 4
SSM_HPG = 4
SSM_STATE = 64
SSM_BC = SSM_GROUPS * SSM_STATE
SSM_CONV = 5
SSM_XBC = SSM_INNER + 2 * SSM_BC
D_FF = 2816
EPS = 1e-6
LOG2E = math.log2(math.e)
IN_SIZES = (GLA_QK, GLA_QK, GLA_V, GLA_V, GLA_GATE_RANK, GLA_GATE_RANK,
            SSM_INNER, SSM_XBC, SSM_HEADS, SSM_HEADS, D_MODEL, D_MODEL)

SMALL_W = 128
LANE_RF = 0
LANE_RB = 16
LANE_DTF = 32
LANE_DTB = 48

assert GLA_HEADS == SSM_GROUPS

LANES = 128
CHUNK = 128
HALO = 8
VMEM_LIMIT = 56 * 1024 * 1024


def _dot(a, b):
    return jnp.dot(a, b, preferred_element_type=F32)


def _dot_nt(a, b):
    return lax.dot_general(a, b, (((1,), (1,)), ((), ())), preferred_element_type=F32)


def _dot_tn(a, b):
    return lax.dot_general(a, b, (((0,), (0,)), ((), ())), preferred_element_type=F32)


def _split2(x):
    hi = x.astype(BF16)
    return hi, (x - hi.astype(F32)).astype(BF16)


def _dot_exact_lhs(m, x):
    hi, lo = _split2(x)
    return _dot(m, hi) + _dot(m, lo)


def _rms(x, w):
    return x * lax.rsqrt(jnp.mean(x * x, axis=-1, keepdims=True) + EPS) * w


def _sigmoid_of_half(h):
    return 0.5 + 0.5 * jnp.tanh(h)


def _silu_of_half(h):
    return h + h * jnp.tanh(h)


def _silu(x):
    return _silu_of_half(0.5 * x)


def _softplus(x):
    return jnp.maximum(x, 0.0) + jnp.log1p(jnp.exp(-jnp.abs(x)))


def _tri_mask(n):
    row = lax.broadcasted_iota(jnp.int32, (n, n), 0)
    col = lax.broadcasted_iota(jnp.int32, (n, n), 1)
    return col <= row


def _lane_block(width, block):
    return lax.broadcasted_iota(jnp.int32, (1, width), 1) // block


def _stack_row(row):
    return jnp.concatenate(
        [jnp.broadcast_to(row[:, g * 256:(g + 1) * 256], (SSM_STATE, 256)) for g in range(SSM_GROUPS)], axis=0)


def _inproj_kernel(tiles_per_row, x_ref, xp_ref, xn_ref, nw_ref,
                   wqk_ref, wv_ref, wog_ref, wz_ref, wxbc_ref, wgg_ref, wgs_ref, wsm_ref,
                   upc_ref, gbias_ref, dtbias_ref, alog_ref, cwh_ref, cbh_ref,
                   qk_ref, v_ref, sog_ref, sz_ref, xa_ref, sgg_ref, sgs_ref, lg_ref, la_ref, ldt_ref,
                   xp_s):
    i = pl.program_id(0)
    tm = x_ref.shape[0]
    nw = nw_ref[...]
    u = _rms(x_ref[...], nw).astype(BF16)

    pos = i % tiles_per_row
    has_prev = jnp.where(pos > 0, 1.0, 0.0)
    has_next = jnp.where(pos < tiles_per_row - 1, 1.0, 0.0)
    xh = jnp.concatenate([xp_ref[...], xn_ref[...]], axis=0)
    xbc_h = _dot(_rms(xh, nw).astype(BF16), wxbc_ref[...])
    xbc = _dot(u, wxbc_ref[...])
    n_slab = SSM_XBC // LANES
    for j in range(n_slab):
        cols = slice(j * LANES, (j + 1) * LANES)
        xp_s[j, HALO:HALO + tm, :] = xbc[:, cols]
        xp_s[j, 0:HALO, :] = xbc_h[0:HALO, cols] * has_prev
        xp_s[j, HALO + tm:2 * HALO + tm, :] = xbc_h[HALO:2 * HALO, cols] * has_next

    def conv_slab(j):
        cols = slice(j * LANES, (j + 1) * LANES)
        acc = cbh_ref[:, cols] + cwh_ref[0:1, cols] * xp_s[j, HALO - SSM_CONV // 2:HALO - SSM_CONV // 2 + tm, :]
        for tap in range(1, SSM_CONV):
            off = HALO + tap - SSM_CONV // 2
            acc = acc + cwh_ref[tap:tap + 1, cols] * xp_s[j, off:off + tm, :]
        xa_ref[:, cols] = _silu_of_half(acc).astype(BF16)

    sm = _dot(u, wsm_ref[...])
    xg = _dot(sm.astype(BF16), upc_ref[...]) + gbias_ref[...]
    lg_ref[...] = (jnp.minimum(xg, 0.0) * (LOG2E / GLA_GATE_NORM)
                   - jnp.log2(1.0 + jnp.exp2(jnp.abs(xg) * (-LOG2E))) * (1.0 / GLA_GATE_NORM))
    dt = _softplus(sm + dtbias_ref[...])
    lane = lax.broadcasted_iota(jnp.int32, (1, SMALL_W), 1)
    live = (lane >= LANE_DTF) & (lane < LANE_DTB + SSM_HEADS)
    la_ref[...] = dt * jnp.where(live, -LOG2E * jnp.exp(alog_ref[...]), 0.0)
    ldt_ref[...] = jnp.log2(dt)

    groups = [
        lambda: qk_ref.__setitem__(..., _dot(u, wqk_ref[...]).astype(BF16)),
        lambda: v_ref.__setitem__(..., _dot(u, wv_ref[...]).astype(BF16)),
        lambda: sog_ref.__setitem__(..., _silu_of_half(_dot(u, wog_ref[...])).astype(BF16)),
        lambda: sz_ref.__setitem__(..., _silu_of_half(_dot(u, wz_ref[...])).astype(BF16)),
        lambda: sgg_ref.__setitem__(..., _sigmoid_of_half(_dot(u, wgg_ref[...])).astype(BF16)),
        lambda: sgs_ref.__setitem__(..., _sigmoid_of_half(_dot(u, wgs_ref[...])).astype(BF16)),
    ]
    per = n_slab // len(groups)
    for gi, emit in enumerate(groups):
        for j in range(gi * per, (gi + 1) * per):
            conv_slab(j)
        emit()


def _const_spec(shape):
    nd = len(shape)
    return pl.BlockSpec(shape, lambda *_: (0,) * nd, pipeline_mode=pl.Buffered(1))


def _inproj(x2d, seq_len, nw, ws, consts, tm):
    n = x2d.shape[0]
    hb = tm // HALO
    nh = n // HALO
    widths = [2 * GLA_QK, GLA_V, GLA_V, SSM_INNER, SSM_XBC, D_MODEL, D_MODEL, 2 * GLA_QK, SMALL_W, SMALL_W]
    dtypes = [BF16] * 7 + [F32] * 3
    row = lambda i: (i, 0)
    return pl.pallas_call(
        functools.partial(_inproj_kernel, seq_len // tm),
        grid=(n // tm,),
        in_specs=[pl.BlockSpec((tm, D_MODEL), row),
                  pl.BlockSpec((HALO, D_MODEL), lambda i: (jnp.maximum(i * hb - 1, 0), 0)),
                  pl.BlockSpec((HALO, D_MODEL), lambda i: (jnp.minimum((i + 1) * hb, nh - 1), 0)),
                  _const_spec(nw.shape)]
                 + [_const_spec(w.shape) for w in ws] + [_const_spec(a.shape) for a in consts],
        out_specs=[pl.BlockSpec((tm, w), row) for w in widths],
        out_shape=[jax.ShapeDtypeStruct((n, w), dt) for w, dt in zip(widths, dtypes)],
        scratch_shapes=[pltpu.VMEM((SSM_XBC // LANES, tm + 2 * HALO, LANES), F32)],
        compiler_params=pltpu.CompilerParams(dimension_semantics=("arbitrary",), vmem_limit_bytes=VMEM_LIMIT),
        name="inproj",
    )(x2d, x2d, x2d, nw, *ws, *consts)


def _bwd_state_kernel(k_ref, v_ref, xa_ref, lgb_ref, la_ref, ldt_ref, p2b_ref,
                      sgb_ref, ssb_ref,
                      sg_s, ss_s):
    s = pl.program_id(1)
    t = k_ref.shape[0]
    nc = t // CHUNK

    @pl.when(s == 0)
    def _():
        sg_s[...] = jnp.zeros_like(sg_s)
        ss_s[...] = jnp.zeros_like(ss_s)

    sgb_ref[...] = sg_s[...]
    ssb_ref[...] = ss_s[...]

    zz = jnp.concatenate([lgb_ref[...], la_ref[...]], axis=1)
    lmat = jnp.where(_tri_mask(CHUNK), 1.0, 0.0).astype(BF16)
    carry = jnp.zeros((1, GLA_QK + SMALL_W), F32)
    parts = []
    for c in range(nc):
        cs = _dot_exact_lhs(lmat, zz[c * CHUNK:(c + 1) * CHUNK]) + carry
        parts.append(cs)
        carry = cs[CHUNK - 1:CHUNK, :]
    ics = jnp.concatenate(parts, axis=0)
    tot = carry
    ecs = ics - zz

    kbs = (k_ref[...].astype(F32) * jnp.exp2(ecs[:, :GLA_QK])).astype(BF16)
    head_of_lane = _lane_block(GLA_QK, GLA_DK)
    new_g = sg_s[...] * jnp.exp2(tot[:, :GLA_QK])
    vt_k = _dot_tn(v_ref[...], kbs)
    for h in range(GLA_HEADS):
        new_g = new_g + jnp.where(head_of_lane == h, vt_k[h * GLA_DV:(h + 1) * GLA_DV, :], 0.0)
    sg_s[...] = new_g

    p2b = p2b_ref[...]
    wb = jnp.exp2(ecs[:, GLA_QK:] + ldt_ref[...])
    et_hi, et_lo = _split2(jnp.broadcast_to(jnp.exp2(tot[:, GLA_QK:]), (8, SMALL_W)))
    exp_all = _dot(jnp.concatenate([wb.astype(BF16), et_hi, et_lo], axis=0), p2b)
    wexp = exp_all[0:t, :]
    dec = _stack_row(exp_all[t:t + 1, :] + exp_all[t + 8:t + 9, :])
    xw = (xa_ref[:, 0:SSM_INNER].astype(F32) * wexp).astype(BF16)
    contrib = jnp.concatenate(
        [_dot_tn(xa_ref[:, SSM_INNER + g * SSM_STATE:SSM_INNER + (g + 1) * SSM_STATE], xw[:, g * 256:(g + 1) * 256])
         for g in range(SSM_GROUPS)], axis=0)
    ss_s[...] = ss_s[...] * dec + contrib


def _bwd_states(qk, v, xa, lg, la, ldt, p2b, t):
    b, l, _ = qk.shape
    ns = l // t
    rev = lambda i, s: (i, ns - 1 - s, 0)
    rev1 = lambda i, s: (i, ns - 1 - s, 1)
    return pl.pallas_call(
        _bwd_state_kernel,
        grid=(b, ns),
        in_specs=[
            pl.BlockSpec((None, t, GLA_QK), rev1),
            pl.BlockSpec((None, t, GLA_V), rev),
            pl.BlockSpec((None, t, SSM_INNER + SSM_BC), rev),
            pl.BlockSpec((None, t, GLA_QK), rev1),
            pl.BlockSpec((None, t, SMALL_W), rev),
            pl.BlockSpec((None, t, SMALL_W), rev),
            _const_spec(p2b.shape),
        ],
        out_specs=[
            pl.BlockSpec((None, None, GLA_DV, GLA_QK), lambda i, s: (i, ns - 1 - s, 0, 0)),
            pl.BlockSpec((None, None, SSM_BC, 256), lambda i, s: (i, ns - 1 - s, 0, 0)),
        ],
        out_shape=[
            jax.ShapeDtypeStruct((b, ns, GLA_DV, GLA_QK), F32),
            jax.ShapeDtypeStruct((b, ns, SSM_BC, 256), F32),
        ],
        scratch_shapes=[pltpu.VMEM((GLA_DV, GLA_QK), F32), pltpu.VMEM((SSM_BC, 256), F32)],
        compiler_params=pltpu.CompilerParams(dimension_semantics=("arbitrary", "arbitrary"),
                                             vmem_limit_bytes=VMEM_LIMIT),
        name="bwd_states",
    )(qk, v, xa, lg, la, ldt, p2b)


def _mixer_kernel(x_ref, qk_ref, v_ref, sog_ref, sz_ref, xa_ref, sgg_ref, sgs_ref, lg_ref, la_ref, ldt_ref,
                  sgb_ref, ssb_ref,
                  glanw_ref, dskip_ref, ssmnw_ref, p2_ref, wbg_ref, wbs_ref, wo_ref,
                  out_ref,
                  sgf_s, ssf_s, af_s, rb_s, qf_s, qb_s, kf_s, kb_s, qfi_s, qbi_s, etf_s, etb_s,
                  kvf_s, kvb_s, stf_s, stb_s, dsf_s, dsb_s, decf_s, decb_s, ssf_c, ssb_c, o_s, *, ns):
    g = pl.program_id(0)
    t = x_ref.shape[0]
    nc = t // CHUNK
    C = CHUNK

    @pl.when(g % ns == 0)
    def _():
        sgf_s[...] = jnp.zeros_like(sgf_s)
        ssf_s[...] = jnp.zeros_like(ssf_s)

    @pl.when(g == 0)
    def _():
        o_s[...] = jnp.zeros_like(o_s)

    slot_w = g % 2
    slot_r = 1 - slot_w

    def finish_norms(part, nparts):
        rr = slice(part * (t // nparts), (part + 1) * (t // nparts))
        parts = []
        for h in range(GLA_HEADS):
            oh = o_s[slot_r, rr, h * GLA_DV:(h + 1) * GLA_DV]
            parts.append(oh * lax.rsqrt(jnp.mean(oh * oh, axis=-1, keepdims=True) + EPS))
        gl = (jnp.concatenate(parts, axis=1) * glanw_ref[...] * sog_ref[rr, :].astype(F32)).astype(BF16)
        y = _rms(o_s[slot_r, rr, GLA_V:] * sz_ref[rr, :].astype(F32), ssmnw_ref[...]).astype(BF16)
        return gl, y

    def finish_merge(part, nparts, gl, y):
        rr = slice(part * (t // nparts), (part + 1) * (t // nparts))
        m = (sgg_ref[rr, :].astype(F32) * _dot(gl, wbg_ref[...])
             + sgs_ref[rr, :].astype(F32) * _dot(y, wbs_ref[...]))
        out_ref[rr, :] = x_ref[rr, :] + _dot(m.astype(BF16), wo_ref[...])

    tri = _tri_mask(C)
    lmat = jnp.where(tri, 1.0, 0.0).astype(BF16)
    head_of_lane = _lane_block(GLA_QK, GLA_DK)
    group_of_lane = _lane_block(SSM_BC, SSM_STATE)
    lane128 = lax.broadcasted_iota(jnp.int32, (1, SMALL_W), 1)
    fwd_lane = lane128 < LANE_DTB
    lane_lo = lane128 < SSM_HEAD_DIM
    p2 = p2_ref[...]
    zero_bf = jnp.zeros((), BF16)

    def chunk_a(c):
        rows = slice(c * C, (c + 1) * C)
        la = la_ref[rows, :]
        lg = lg_ref[rows, :]
        cs = _dot_exact_lhs(lmat, jnp.concatenate([lg, la], axis=1))
        ics, ics_b, af = cs[:, :GLA_QK], cs[:, GLA_QK:2 * GLA_QK], cs[:, 2 * GLA_QK:]
        rcs = ics_b[C - 1:C, :] - ics_b + lg[:, GLA_QK:]
        rb = af[C - 1:C, :] - af + la
        af_s[rows, :] = af
        rb_s[rows, :] = rb

        m_f = ics[C // 2 - 1:C // 2, :]
        tot_f = ics[C - 1:C, :]
        m_b = rcs[C // 2:C // 2 + 1, :]
        tot_b = rcs[0:1, :]
        q = qk_ref[rows, 0:GLA_QK].astype(F32)
        k = qk_ref[rows, GLA_QK:2 * GLA_QK].astype(F32)
        qf32 = q * jnp.exp2(ics - m_f)
        kf32 = k * jnp.exp2(m_f - ics)
        qb32 = q * jnp.exp2(rcs - m_b)
        kb32 = k * jnp.exp2(m_b - rcs)
        kfs = (kf32 * jnp.exp2(tot_f - m_f)).astype(BF16)
        kbs = (kb32 * jnp.exp2(tot_b - m_b)).astype(BF16)
        qf_s[rows, :] = qf32.astype(BF16)
        qb_s[rows, :] = qb32.astype(BF16)
        kf_s[rows, :] = kf32.astype(BF16)
        kb_s[rows, :] = kb32.astype(BF16)
        qfi_s[rows, :] = (qf32 * jnp.exp2(m_f)).astype(BF16)
        qbi_s[rows, :] = (qb32 * jnp.exp2(m_b)).astype(BF16)
        etf_s[c] = jnp.broadcast_to(jnp.exp2(tot_f), (8, GLA_QK))
        etb_s[c] = jnp.broadcast_to(jnp.exp2(tot_b), (8, GLA_QK))
        vt_kf = _dot_tn(v_ref[rows, :], kfs)
        vt_kb = _dot_tn(v_ref[rows, :], kbs)
        kvf = jnp.zeros((GLA_DV, GLA_QK), F32)
        kvb = jnp.zeros((GLA_DV, GLA_QK), F32)
        for h in range(GLA_HEADS):
            hm = head_of_lane == h
            kvf = kvf + jnp.where(hm, vt_kf[h * GLA_DV:(h + 1) * GLA_DV, :], 0.0)
            kvb = kvb + jnp.where(hm, vt_kb[h * GLA_DV:(h + 1) * GLA_DV, :], 0.0)
        kvf_s[c] = kvf
        kvb_s[c] = kvb

        tot_a = af[C - 1:C, :]
        tot_r = rb[0:1, :]
        sw = jnp.exp2(jnp.where(fwd_lane, tot_a - af, tot_r - rb) + ldt_ref[rows, :])
        return sw.astype(BF16), jnp.exp2(jnp.where(fwd_lane, tot_a, tot_r))

    def chunk_a2(c, swx):
        rows = slice(c * C, (c + 1) * C)
        xsf = xa_ref[rows, 0:SSM_INNER].astype(F32)
        bm = xa_ref[rows, SSM_INNER:SSM_INNER + SSM_BC]
        xwf = (xsf * swx[:, :SSM_INNER].astype(F32)).astype(BF16)
        xwb = (xsf * swx[:, SSM_INNER:].astype(F32)).astype(BF16)
        bms = [bm[:, g * SSM_STATE:(g + 1) * SSM_STATE] for g in range(SSM_GROUPS)]
        dsf_s[c] = jnp.concatenate(
            [_dot_tn(bms[g], xwf[:, g * 256:(g + 1) * 256]) for g in range(SSM_GROUPS)], axis=0)
        dsb_s[c] = jnp.concatenate(
            [_dot_tn(bms[g], xwb[:, g * 256:(g + 1) * 256]) for g in range(SSM_GROUPS)], axis=0)

    n_fin = 2 if nc % 2 == 0 else 1
    fin_in = [finish_norms(p_, n_fin) for p_ in range(n_fin)]
    sws, ets = [], []
    for c in range(nc):
        sw_c, et_c = chunk_a(c)
        sws.append(sw_c)
        ets.append(et_c)
        if n_fin == 2 and c == nc // 2 - 1:
            finish_merge(0, 2, *fin_in[0])
    et_hi, et_lo = _split2(jnp.concatenate(ets + [jnp.zeros((8 - nc, SMALL_W), F32)], axis=0))
    exp_all = _dot(jnp.concatenate(sws + [et_hi, et_lo], axis=0), p2)
    swx_all = exp_all[0:t, :].astype(BF16)
    etx = exp_all[t:t + 8, :] + exp_all[t + 8:t + 16, :]
    for c in range(nc):
        chunk_a2(c, swx_all[c * C:(c + 1) * C, :])
        decf_s[c] = _stack_row(etx[c:c + 1, :SSM_INNER])
        decb_s[c] = _stack_row(etx[c:c + 1, SSM_INNER:])
        if c == nc // 2 - 1 or nc == 1:
            finish_merge(n_fin - 1, n_fin, *fin_in[n_fin - 1])

    sg = sgf_s[...]
    ss = ssf_s[...]
    for c in range(nc):
        stf_s[c] = sg.astype(BF16)
        ssf_c[c] = ss.astype(BF16)
        sg = sg * etf_s[c][0:1, :] + kvf_s[c]
        ss = ss * decf_s[c] + dsf_s[c]
    sgf_s[...] = sg
    ssf_s[...] = ss
    sg = sgb_ref[...]
    ss = ssb_ref[...]
    for c in reversed(range(nc)):
        stb_s[c] = sg.astype(BF16)
        ssb_c[c] = ss.astype(BF16)
        sg = sg * etb_s[c][0:1, :] + kvb_s[c]
        ss = ss * decb_s[c] + dsb_s[c]

    tri4 = jnp.concatenate([tri] * 4, axis=1)

    def chunk_o(c):
        rows = slice(c * C, (c + 1) * C)
        kf = kf_s[rows, :]
        kb = kb_s[rows, :]
        kf_m = jnp.concatenate([jnp.where(head_of_lane == h, kf, zero_bf) for h in range(GLA_HEADS)], axis=0)
        kb_m = jnp.concatenate([jnp.where(head_of_lane == h, kb, zero_bf) for h in range(GLA_HEADS)], axis=0)
        att = jnp.where(tri4, _dot_nt(qf_s[rows, :], kf_m), _dot_nt(qb_s[rows, :], kb_m)).astype(BF16)
        stf = stf_s[c]
        stb = stb_s[c]

        def gla_head(h):
            blk = slice((h // 2) * LANES, (h // 2 + 1) * LANES)
            hm = (lane128 // GLA_DK) == (h % 2)
            qi = jnp.concatenate([jnp.where(hm, qfi_s[rows, blk], zero_bf),
                                  jnp.where(hm, qbi_s[rows, blk], zero_bf)], axis=1)
            st = jnp.concatenate([stf[:, blk], stb[:, blk]], axis=1)
            vh = v_ref[rows, h * GLA_DV:(h + 1) * GLA_DV]
            o_s[slot_w, rows, h * GLA_DV:(h + 1) * GLA_DV] = _dot(att[:, h * C:(h + 1) * C], vh) + _dot_nt(qi, st)

        xs = xa_ref[rows, 0:SSM_INNER]
        bm = xa_ref[rows, SSM_INNER:SSM_INNER + SSM_BC]
        cm = xa_ref[rows, SSM_INNER + SSM_BC:SSM_XBC]
        af = af_s[rows, :]
        rb = rb_s[rows, :]
        ldt_t = ldt_ref[rows, :].T
        arow_t = af.T - ldt_t
        rrow_t = rb.T - ldt_t
        bm_m = jnp.concatenate([jnp.where(group_of_lane == g, bm, zero_bf) for g in range(SSM_GROUPS)], axis=0)
        cbm_all = _dot_nt(cm, bm_m)
        cm32 = cm.astype(F32)
        ssf = ssf_c[c]
        ssb = ssb_c[c]
        for g in range(SSM_GROUPS):
            gla_head(g)
            cbm = cbm_all[:, g * C:(g + 1) * C]
            cblk = cm32[:, (g // 2) * LANES:(g // 2 + 1) * LANES]
            crot = pltpu.roll(cblk, SSM_STATE, 1)
            cm2 = jnp.where(lane_lo, cblk, crot) if g % 2 == 0 else jnp.where(lane_lo, crot, cblk)
            srow = slice(g * SSM_STATE, (g + 1) * SSM_STATE)
            for pair in range(SSM_HPG // 2):
                blk = (g * SSM_HPG + pair * 2) * SSM_HEAD_DIM
                pcol = slice(pair * LANES, (pair + 1) * LANES)
                rhs = jnp.concatenate([xs[:, blk:blk + LANES], ssf[srow, pcol], ssb[srow, pcol]], axis=0)
                lhs = []
                for jj in range(2):
                    hd = g * SSM_HPG + pair * 2 + jj
                    lf = LANE_DTF + hd
                    lb = LANE_DTB + hd
                    a_col = jnp.broadcast_to(af[:, lf:lf + 1], (C, C))
                    r_col = jnp.broadcast_to(rb[:, lb:lb + 1], (C, C))
                    diff = jnp.where(tri, a_col - arow_t[lf:lf + 1, :], r_col - rrow_t[lb:lb + 1, :])
                    mm = (cbm * jnp.exp2(diff)).astype(BF16)
                    ecol = jnp.exp2(jnp.where(lane_lo, a_col, r_col))
                    lhs.append(jnp.concatenate([mm, (cm2 * ecol).astype(BF16)], axis=1))
                yy = _dot(jnp.concatenate(lhs, axis=0), rhs)
                xpair = xs[:, blk:blk + LANES].astype(F32)
                o_s[slot_w, rows, GLA_V + blk:GLA_V + blk + LANES] = (
                    jnp.where(lane_lo, yy[0:C, :], yy[C:2 * C, :]) + dskip_ref[:, blk:blk + LANES] * xpair)

    for c in range(nc):
        chunk_o(c)


def _mixer(x, qk, v, sog, sz, xa, sgg, sgs, lg, la, ldt, sgb, ssb, consts, seq_len, t):
    n = x.shape[0]
    ns = seq_len // t
    nsteps = n // t
    nc = t // CHUNK
    cur = lambda g: (jnp.minimum(g, nsteps - 1), 0)
    lag = lambda g: (jnp.maximum(g - 1, 0), 0)
    tok = lambda w: pl.BlockSpec((t, w), cur)
    tok_lag = lambda w: pl.BlockSpec((t, w), lag)
    st = lambda r, c: pl.BlockSpec((None, r, c), lambda g: (jnp.minimum(g, nsteps - 1), 0, 0))
    scratch = [
        pltpu.VMEM((GLA_DV, GLA_QK), F32),
        pltpu.VMEM((SSM_BC, 256), F32),
        pltpu.VMEM((t, SMALL_W), F32),
        pltpu.VMEM((t, SMALL_W), F32),
        pltpu.VMEM((t, GLA_QK), BF16),
        pltpu.VMEM((t, GLA_QK), BF16),
        pltpu.VMEM((t, GLA_QK), BF16),
        pltpu.VMEM((t, GLA_QK), BF16),
        pltpu.VMEM((t, GLA_QK), BF16),
        pltpu.VMEM((t, GLA_QK), BF16),
        pltpu.VMEM((nc, 8, GLA_QK), F32),
        pltpu.VMEM((nc, 8, GLA_QK), F32),
        pltpu.VMEM((nc, GLA_DV, GLA_QK), F32),
        pltpu.VMEM((nc, GLA_DV, GLA_QK), F32),
        pltpu.VMEM((nc, GLA_DV, GLA_QK), BF16),
        pltpu.VMEM((nc, GLA_DV, GLA_QK), BF16),
        pltpu.VMEM((nc, SSM_BC, 256), F32),
        pltpu.VMEM((nc, SSM_BC, 256), F32),
        pltpu.VMEM((nc, SSM_BC, 256), F32),
        pltpu.VMEM((nc, SSM_BC, 256), F32),
        pltpu.VMEM((nc, SSM_BC, 256), BF16),
        pltpu.VMEM((nc, SSM_BC, 256), BF16),
        pltpu.VMEM((2, t, GLA_V + SSM_INNER), F32),
    ]
    return pl.pallas_call(
        functools.partial(_mixer_kernel, ns=ns),
        grid=(nsteps + 1,),
        in_specs=[tok_lag(D_MODEL), tok(2 * GLA_QK), tok(GLA_V), tok_lag(GLA_V), tok_lag(SSM_INNER), tok(SSM_XBC),
                  tok_lag(D_MODEL), tok_lag(D_MODEL), tok(2 * GLA_QK), tok(SMALL_W), tok(SMALL_W),
                  st(GLA_DV, GLA_QK), st(SSM_BC, 256)]
                 + [_const_spec(a.shape) for a in consts],
        out_specs=tok_lag(D_MODEL),
        out_shape=jax.ShapeDtypeStruct((n, D_MODEL), F32),
        scratch_shapes=scratch,
        compiler_params=pltpu.CompilerParams(dimension_semantics=("arbitrary",), vmem_limit_bytes=VMEM_LIMIT),
        name="mixer",
    )(x, qk, v, sog, sz, xa, sgg, sgs, lg, la, ldt, sgb, ssb, *consts)


FFN_COL_CHUNKS = ((0, 768), (768, 1536), (1536, 2304), (2304, D_FF))


def _ffn_kernel(x_ref, nw_ref, wg_ref, wu_ref, wd_ref, fw_ref, out_ref):
    x = x_ref[...]
    h = _rms(x, nw_ref[...]).astype(BF16)
    y = x
    for lo, hi in FFN_COL_CHUNKS:
        a = (_silu(_dot(h, wg_ref[:, lo:hi])) * _dot(h, wu_ref[:, lo:hi])).astype(BF16)
        y = y + _dot(a, wd_ref[lo:hi, :])
    out_ref[...] = _rms(y, fw_ref[...])


def _ffn(x2d, nw, wg, wu, wd, fw, tm):
    n = x2d.shape[0]
    row = lambda i: (i, 0)
    return pl.pallas_call(
        _ffn_kernel,
        grid=(n // tm,),
        in_specs=[pl.BlockSpec((tm, D_MODEL), row)] + [_const_spec(a.shape) for a in (nw, wg, wu, wd, fw)],
        out_specs=pl.BlockSpec((tm, D_MODEL), row),
        out_shape=jax.ShapeDtypeStruct((n, D_MODEL), F32),
        compiler_params=pltpu.CompilerParams(dimension_semantics=("arbitrary",), vmem_limit_bytes=VMEM_LIMIT),
        name="ffn",
    )(x2d, nw, wg, wu, wd, fw)


def _tile(n, pref):
    t = pref
    while n % t:
        t //= 2
    return t


def _prep_weights(norm_mix_w, w_in, gla_up_f, gla_bias_f, gla_up_b, gla_bias_b, gla_norm_w, conv_w, conv_b,
                  dt_bias_f, dt_bias_b, a_log_f, a_log_b, d_skip, ssm_norm_w, w_br_gla, w_br_ssm, w_out,
                  norm_ffn_w, w_ffn_gate, w_ffn_up, w_ffn_down, norm_final_w):
    offs = [0]
    for sz in IN_SIZES:
        offs.append(offs[-1] + sz)
    w_in = w_in.astype(BF16)
    col = lambda i: w_in[:, offs[i]:offs[i + 1]]
    wq, wk, wv, wog, wrf, wrb, wz, wxbc, wdtf, wdtb, wgg, wgs = [col(i) for i in range(12)]
    wqk = jnp.concatenate([wq * (GLA_DK ** -0.5), wk], axis=1)
    wsm = jnp.concatenate([wrf, wrb, wdtf, wdtb, jnp.zeros((D_MODEL, SMALL_W - 64), BF16)], axis=1)
    proj_ws = [w.astype(BF16) for w in (wqk, wv, 0.5 * wog, 0.5 * wz, wxbc, 0.5 * wgg, 0.5 * wgs, wsm)]

    upc = jnp.zeros((SMALL_W, 2 * GLA_QK), F32)
    upc = upc.at[LANE_RF:LANE_RF + GLA_GATE_RANK, :GLA_QK].set(gla_up_f)
    upc = upc.at[LANE_RB:LANE_RB + GLA_GATE_RANK, GLA_QK:].set(gla_up_b)
    upc = upc.astype(BF16)
    gbias = jnp.concatenate([gla_bias_f, gla_bias_b])[None, :]

    def small_row(f, b):
        r = jnp.zeros((1, SMALL_W), F32)
        r = r.at[0, LANE_DTF:LANE_DTF + SSM_HEADS].set(f)
        return r.at[0, LANE_DTB:LANE_DTB + SSM_HEADS].set(b)

    dtbias = small_row(dt_bias_f, dt_bias_b)
    alog = small_row(a_log_f, a_log_b)

    lane = jnp.arange(SMALL_W)[:, None]
    blk = jnp.arange(SSM_INNER)[None, :] // SSM_HEAD_DIM
    p2f = (lane == LANE_DTF + blk)
    p2b = (lane == LANE_DTB + blk)
    p2 = jnp.concatenate([p2f, p2b], axis=1).astype(BF16)
    p2b = p2b.astype(BF16)

    cwh = 0.5 * jnp.concatenate([conv_w, jnp.zeros((8 - SSM_CONV, SSM_XBC), F32)], axis=0)
    return dict(
        nw=norm_mix_w[None, :], proj_ws=proj_ws,
        proj_consts=(upc, gbias, dtbias, alog, cwh, 0.5 * conv_b[None, :]),
        p2=p2, p2b=p2b,
        glanw=jnp.tile(gla_norm_w, GLA_HEADS)[None, :], dskip=jnp.repeat(d_skip, SSM_HEAD_DIM)[None, :],
        ssmnw=ssm_norm_w[None, :], wbg=w_br_gla.astype(BF16), wbs=w_br_ssm.astype(BF16),
        wo=w_out.astype(BF16), fnw=norm_ffn_w[None, :], wg=w_ffn_gate.astype(BF16),
        wu=w_ffn_up.astype(BF16), wd=w_ffn_down.astype(BF16), finw=norm_final_w[None, :])


def _run(x, p):
    b, l, _ = x.shape
    n = b * l
    t = _tile(l, 512)
    x2d = x.reshape(n, D_MODEL)
    qk, v, sog, sz, xa, sgg, sgs, lg, la, ldt = _inproj(x2d, l, p["nw"], p["proj_ws"], p["proj_consts"], t)
    r3 = lambda a: a.reshape(b, l, a.shape[-1])
    sgb, ssb = _bwd_states(r3(qk), r3(v), r3(xa), r3(lg), r3(la), r3(ldt), p["p2b"], t)
    sgb = sgb.reshape(n // t, GLA_DV, GLA_QK)
    ssb = ssb.reshape(n // t, SSM_BC, 256)
    consts = (p["glanw"], p["dskip"], p["ssmnw"], p["p2"], p["wbg"], p["wbs"], p["wo"])
    xm = _mixer(x2d, qk, v, sog, sz, xa, sgg, sgs, lg, la, ldt, sgb, ssb, consts, l, t)
    y = _ffn(xm, p["fnw"], p["wg"], p["wu"], p["wd"], p["finw"], _tile(n, 1024))
    return y.reshape(b, l, D_MODEL)


def kernel(x_prompt, x_sample, norm_mix_w, w_in, gla_up_f, gla_bias_f, gla_up_b, gla_bias_b, gla_norm_w, conv_w,
           conv_b, dt_bias_f, dt_bias_b, a_log_f, a_log_b, d_skip, ssm_norm_w, w_br_gla, w_br_ssm, w_out,
           norm_ffn_w, w_ffn_gate, w_ffn_up, w_ffn_down, norm_final_w):
    assert norm_mix_w.shape[0] == 1, "single-layer block"
    p = _prep_weights(norm_mix_w[0], w_in.reshape(w_in.shape[1:]), gla_up_f[0], gla_bias_f[0], gla_up_b[0], gla_bias_b[0],
                      gla_norm_w[0], conv_w[0], conv_b[0], dt_bias_f[0], dt_bias_b[0], a_log_f[0], a_log_b[0],
                      d_skip[0], ssm_norm_w[0], w_br_gla[0], w_br_ssm[0], w_out[0], norm_ffn_w[0],
                      w_ffn_gate[0], w_ffn_up[0], w_ffn_down[0], norm_final_w)
    return (_run(x_prompt, p), _run(x_sample, p))
```

```python
import functools
import math

import jax
import jax.numpy as jnp
from jax import lax
from jax.experimental import pallas as pl
from jax.experimental.pallas import tpu as pltpu

F32 = jnp.float32
BF16 = jnp.bfloat16

D_MODEL = 1024
GLA_HEADS = 4
GLA_DK = 64
GLA_DV = 128
GLA_QK = GLA_HEADS * GLA_DK
GLA_V = GLA_HEADS * GLA_DV
GLA_GATE_RANK = 16
GLA_GATE_NORM = 16.0
SSM_HEAD_DIM = 64
SSM_INNER = 1024
SSM_HEADS = 16
SSM_GROUPS = 4
SSM_HPG = 4
SSM_STATE = 64
SSM_BC = SSM_GROUPS * SSM_STATE
SSM_CONV = 5
SSM_XBC = SSM_INNER + 2 * SSM_BC
D_FF = 2816
EPS = 1e-6
LOG2E = math.log2(math.e)
IN_SIZES = (GLA_QK, GLA_QK, GLA_V, GLA_V, GLA_GATE_RANK, GLA_GATE_RANK,
            SSM_INNER, SSM_XBC, SSM_HEADS, SSM_HEADS, D_MODEL, D_MODEL)

SMALL_W = 128
LANE_RF = 0
LANE_RB = 16
LANE_DTF = 32
LANE_DTB = 48

assert GLA_HEADS == SSM_GROUPS

LANES = 128
CHUNK = 128
HALO = 8
VMEM_LIMIT = 56 * 1024 * 1024


def _dot(a, b):
    return jnp.dot(a, b, preferred_element_type=F32)


def _dot_nt(a, b):
    return lax.dot_general(a, b, (((1,), (1,)), ((), ())), preferred_element_type=F32)


def _dot_tn(a, b):
    return lax.dot_general(a, b, (((0,), (0,)), ((), ())), preferred_element_type=F32)


def _split2(x):
    hi = x.astype(BF16)
    return hi, (x - hi.astype(F32)).astype(BF16)


def _dot_exact_lhs(m, x):
    hi, lo = _split2(x)
    return _dot(m, hi) + _dot(m, lo)


def _rms(x, w):
    return x * lax.rsqrt(jnp.mean(x * x, axis=-1, keepdims=True) + EPS) * w


def _sigmoid_of_half(h):
    return 0.5 + 0.5 * jnp.tanh(h)


def _silu_of_half(h):
    return h + h * jnp.tanh(h)


def _silu(x):
    return _silu_of_half(0.5 * x)


def _softplus(x):
    return jnp.maximum(x, 0.0) + jnp.log1p(jnp.exp(-jnp.abs(x)))


def _tri_mask(n):
    row = lax.broadcasted_iota(jnp.int32, (n, n), 0)
    col = lax.broadcasted_iota(jnp.int32, (n, n), 1)
    return col <= row


def _lane_block(width, block):
    return lax.broadcasted_iota(jnp.int32, (1, width), 1) // block


def _stack_row(row):
    return jnp.concatenate(
        [jnp.broadcast_to(row[:, g * 256:(g + 1) * 256], (SSM_STATE, 256)) for g in range(SSM_GROUPS)], axis=0)


def _inproj_kernel(tiles_per_row, x_ref, xp_ref, xn_ref, nw_ref,
                   wqk_ref, wv_ref, wog_ref, wz_ref, wxbc_ref, wgg_ref, wgs_ref, wsm_ref,
                   upc_ref, gbias_ref, dtbias_ref, alog_ref, cwh_ref, cbh_ref,
                   qk_ref, v_ref, sog_ref, sz_ref, xa_ref, sgg_ref, sgs_ref, lg_ref, la_ref, ldt_ref,
                   xp_s):
    i = pl.program_id(0)
    tm = x_ref.shape[0]
    nw = nw_ref[...]
    u = _rms(x_ref[...], nw).astype(BF16)

    pos = i % tiles_per_row
    has_prev = jnp.where(pos > 0, 1.0, 0.0)
    has_next = jnp.where(pos < tiles_per_row - 1, 1.0, 0.0)
    xh = jnp.concatenate([xp_ref[...], xn_ref[...]], axis=0)
    xbc_h = _dot(_rms(xh, nw).astype(BF16), wxbc_ref[...])
    xbc = _dot(u, wxbc_ref[...])
    n_slab = SSM_XBC // LANES
    for j in range(n_slab):
        cols = slice(j * LANES, (j + 1) * LANES)
        xp_s[j, HALO:HALO + tm, :] = xbc[:, cols]
        xp_s[j, 0:HALO, :] = xbc_h[0:HALO, cols] * has_prev
        xp_s[j, HALO + tm:2 * HALO + tm, :] = xbc_h[HALO:2 * HALO, cols] * has_next

    def conv_slab(j):
        cols = slice(j * LANES, (j + 1) * LANES)
        acc = cbh_ref[:, cols] + cwh_ref[0:1, cols] * xp_s[j, HALO - SSM_CONV // 2:HALO - SSM_CONV // 2 + tm, :]
        for tap in range(1, SSM_CONV):
            off = HALO + tap - SSM_CONV // 2
            acc = acc + cwh_ref[tap:tap + 1, cols] * xp_s[j, off:off + tm, :]
        xa_ref[:, cols] = _silu_of_half(acc).astype(BF16)

    sm = _dot(u, wsm_ref[...])
    xg = _dot(sm.astype(BF16), upc_ref[...]) + gbias_ref[...]
    lg_ref[...] = (jnp.minimum(xg, 0.0) * (LOG2E / GLA_GATE_NORM)
                   - jnp.log2(1.0 + jnp.exp2(jnp.abs(xg) * (-LOG2E))) * (1.0 / GLA_GATE_NORM))
    dt = _softplus(sm + dtbias_ref[...])
    lane = lax.broadcasted_iota(jnp.int32, (1, SMALL_W), 1)
    live = (lane >= LANE_DTF) & (lane < LANE_DTB + SSM_HEADS)
    la_ref[...] = dt * jnp.where(live, -LOG2E * jnp.exp(alog_ref[...]), 0.0)
    ldt_ref[...] = jnp.log2(dt)

    groups = [
        lambda: qk_ref.__setitem__(..., _dot(u, wqk_ref[...]).astype(BF16)),
        lambda: v_ref.__setitem__(..., _dot(u, wv_ref[...]).astype(BF16)),
        lambda: sog_ref.__setitem__(..., _silu_of_half(_dot(u, wog_ref[...])).astype(BF16)),
        lambda: sz_ref.__setitem__(..., _silu_of_half(_dot(u, wz_ref[...])).astype(BF16)),
        lambda: sgg_ref.__setitem__(..., _sigmoid_of_half(_dot(u, wgg_ref[...])).astype(BF16)),
        lambda: sgs_ref.__setitem__(..., _sigmoid_of_half(_dot(u, wgs_ref[...])).astype(BF16)),
    ]
    per = n_slab // len(groups)
    for gi, emit in enumerate(groups):
        for j in range(gi * per, (gi + 1) * per):
            conv_slab(j)
        emit()


def _const_spec(shape):
    nd = len(shape)
    return pl.BlockSpec(shape, lambda *_: (0,) * nd, pipeline_mode=pl.Buffered(1))


def _inproj(x2d, seq_len, nw, ws, consts, tm):
    n = x2d.shape[0]
    hb = tm // HALO
    nh = n // HALO
    widths = [2 * GLA_QK, GLA_V, GLA_V, SSM_INNER, SSM_XBC, D_MODEL, D_MODEL, 2 * GLA_QK, SMALL_W, SMALL_W]
    dtypes = [BF16] * 7 + [F32] * 3
    row = lambda i: (i, 0)
    return pl.pallas_call(
        functools.partial(_inproj_kernel, seq_len // tm),
        grid=(n // tm,),
        in_specs=[pl.BlockSpec((tm, D_MODEL), row),
                  pl.BlockSpec((HALO, D_MODEL), lambda i: (jnp.maximum(i * hb - 1, 0), 0)),
                  pl.BlockSpec((HALO, D_MODEL), lambda i: (jnp.minimum((i + 1) * hb, nh - 1), 0)),
                  _const_spec(nw.shape)]
                 + [_const_spec(w.shape) for w in ws] + [_const_spec(a.shape) for a in consts],
        out_specs=[pl.BlockSpec((tm, w), row) for w in widths],
        out_shape=[jax.ShapeDtypeStruct((n, w), dt) for w, dt in zip(widths, dtypes)],
        scratch_shapes=[pltpu.VMEM((SSM_XBC // LANES, tm + 2 * HALO, LANES), F32)],
        compiler_params=pltpu.CompilerParams(dimension_semantics=("arbitrary",), vmem_limit_bytes=VMEM_LIMIT),
        name="inproj",
    )(x2d, x2d, x2d, nw, *ws, *consts)


BWD_NBUF = 3


def _bwd_state_kernel(ns, nsteps, qk_hbm, v_hbm, xa_hbm, lg_hbm, la_hbm, ldt_hbm, p2b_ref,
                      sgb_ref, ssb_ref,
                      sg_s, ss_s, kbuf, vbuf, xbuf, lgbuf, labuf, ldtbuf, sem):
    g = pl.program_id(0)
    t = kbuf.shape[1]
    nc = t // CHUNK

    def copies(step, slot):
        row0 = pl.multiple_of(((step // ns) * ns + ns - 1 - step % ns) * t, t)
        rows = pl.ds(row0, t)
        srcs = (qk_hbm.at[rows, pl.ds(GLA_QK, GLA_QK)], v_hbm.at[rows, :], xa_hbm.at[rows, pl.ds(0, SSM_INNER + SSM_BC)],
                lg_hbm.at[rows, pl.ds(GLA_QK, GLA_QK)], la_hbm.at[rows, :], ldt_hbm.at[rows, :])
        bufs = (kbuf, vbuf, xbuf, lgbuf, labuf, ldtbuf)
        return [pltpu.make_async_copy(src, buf.at[slot], sem.at[slot, j])
                for j, (src, buf) in enumerate(zip(srcs, bufs))]

    @pl.when(g == 0)
    def _():
        for d in range(min(BWD_NBUF - 1, nsteps)):
            for cp in copies(d, d):
                cp.start()

    ahead = g + (BWD_NBUF - 1)

    @pl.when(ahead < nsteps)
    def _():
        for cp in copies(ahead, ahead % BWD_NBUF):
            cp.start()

    slot = g % BWD_NBUF
    for cp in copies(g, slot):
        cp.wait()
    k_ref, v_ref, xa_ref = kbuf.at[slot], vbuf.at[slot], xbuf.at[slot]
    lgb_ref, la_ref, ldt_ref = lgbuf.at[slot], labuf.at[slot], ldtbuf.at[slot]

    @pl.when(g % ns == 0)
    def _():
        sg_s[...] = jnp.zeros_like(sg_s)
        ss_s[...] = jnp.zeros_like(ss_s)

    sgb_ref[...] = sg_s[...]
    ssb_ref[...] = ss_s[...]

    zz = jnp.concatenate([lgb_ref[...], la_ref[...]], axis=1)
    lmat = jnp.where(_tri_mask(CHUNK), 1.0, 0.0).astype(BF16)
    carry = jnp.zeros((1, GLA_QK + SMALL_W), F32)
    parts = []
    for c in range(nc):
        cs = _dot_exact_lhs(lmat, zz[c * CHUNK:(c + 1) * CHUNK]) + carry
        parts.append(cs)
        carry = cs[CHUNK - 1:CHUNK, :]
    ics = jnp.concatenate(parts, axis=0)
    tot = carry
    ecs = ics - zz

    kbs = (k_ref[...].astype(F32) * jnp.exp2(ecs[:, :GLA_QK])).astype(BF16)
    head_of_lane = _lane_block(GLA_QK, GLA_DK)
    new_g = sg_s[...] * jnp.exp2(tot[:, :GLA_QK])
    vt_k = _dot_tn(v_ref[...], kbs)
    for h in range(GLA_HEADS):
        new_g = new_g + jnp.where(head_of_lane == h, vt_k[h * GLA_DV:(h + 1) * GLA_DV, :], 0.0)
    sg_s[...] = new_g

    p2b = p2b_ref[...]
    wb = jnp.exp2(ecs[:, GLA_QK:] + ldt_ref[...])
    et_hi, et_lo = _split2(jnp.broadcast_to(jnp.exp2(tot[:, GLA_QK:]), (8, SMALL_W)))
    exp_all = _dot(jnp.concatenate([wb.astype(BF16), et_hi, et_lo], axis=0), p2b)
    wexp = exp_all[0:t, :]
    dec = _stack_row(exp_all[t:t + 1, :] + exp_all[t + 8:t + 9, :])
    xw = (xa_ref[:, 0:SSM_INNER].astype(F32) * wexp).astype(BF16)
    contrib = jnp.concatenate(
        [_dot_tn(xa_ref[:, SSM_INNER + g * SSM_STATE:SSM_INNER + (g + 1) * SSM_STATE], xw[:, g * 256:(g + 1) * 256])
         for g in range(SSM_GROUPS)], axis=0)
    ss_s[...] = ss_s[...] * dec + contrib


def _bwd_states(qk, v, xa, lg, la, ldt, p2b, seq_len, t):
    n = qk.shape[0]
    ns = seq_len // t
    nsteps = n // t
    blk = lambda g: ((g // ns) * ns + ns - 1 - g % ns, 0, 0)
    hbm = pl.BlockSpec(memory_space=pl.ANY)
    ring = lambda w, dt: pltpu.VMEM((BWD_NBUF, t, w), dt)
    return pl.pallas_call(
        functools.partial(_bwd_state_kernel, ns, nsteps),
        grid=(nsteps,),
        in_specs=[hbm] * 6 + [_const_spec(p2b.shape)],
        out_specs=[pl.BlockSpec((None, GLA_DV, GLA_QK), blk), pl.BlockSpec((None, SSM_BC, 256), blk)],
        out_shape=[jax.ShapeDtypeStruct((nsteps, GLA_DV, GLA_QK), F32),
                   jax.ShapeDtypeStruct((nsteps, SSM_BC, 256), F32)],
        scratch_shapes=[pltpu.VMEM((GLA_DV, GLA_QK), F32), pltpu.VMEM((SSM_BC, 256), F32),
                        ring(GLA_QK, BF16), ring(GLA_V, BF16), ring(SSM_INNER + SSM_BC, BF16),
                        ring(GLA_QK, F32), ring(SMALL_W, F32), ring(SMALL_W, F32),
                        pltpu.SemaphoreType.DMA((BWD_NBUF, 6))],
        compiler_params=pltpu.CompilerParams(dimension_semantics=("arbitrary",), vmem_limit_bytes=VMEM_LIMIT),
        name="bwd_states",
    )(qk, v, xa, lg, la, ldt, p2b)


def _mixer_kernel(x_ref, qk_ref, v_ref, sog_ref, sz_ref, xa_ref, sgg_ref, sgs_ref, lg_ref, la_ref, ldt_ref,
                  sgb_ref, ssb_ref,
                  glanw_ref, dskip_ref, ssmnw_ref, p2_ref, wbg_ref, wbs_ref, wo_ref,
                  out_ref,
                  sgf_s, ssf_s, af_s, rb_s, qf_s, qb_s, kf_s, kb_s, qfi_s, qbi_s, etf_s, etb_s,
                  kvf_s, kvb_s, stf_s, stb_s, dsf_s, dsb_s, decf_s, decb_s, ssf_c, ssb_c, o_s, *, ns):
    g = pl.program_id(0)
    t = x_ref.shape[0]
    nc = t // CHUNK
    C = CHUNK

    @pl.when(g % ns == 0)
    def _():
        sgf_s[...] = jnp.zeros_like(sgf_s)
        ssf_s[...] = jnp.zeros_like(ssf_s)

    @pl.when(g == 0)
    def _():
        o_s[...] = jnp.zeros_like(o_s)

    slot_w = g % 2
    slot_r = 1 - slot_w

    def finish_norms(part, nparts):
        rr = slice(part * (t // nparts), (part + 1) * (t // nparts))
        parts = []
        for h in range(GLA_HEADS):
            oh = o_s[slot_r, rr, h * GLA_DV:(h + 1) * GLA_DV]
            parts.append(oh * lax.rsqrt(jnp.mean(oh * oh, axis=-1, keepdims=True) + EPS))
        gl = (jnp.concatenate(parts, axis=1) * glanw_ref[...] * sog_ref[rr, :].astype(F32)).astype(BF16)
        y = _rms(o_s[slot_r, rr, GLA_V:] * sz_ref[rr, :].astype(F32), ssmnw_ref[...]).astype(BF16)
        return gl, y

    def finish_merge(part, nparts, gl, y):
        rr = slice(part * (t // nparts), (part + 1) * (t // nparts))
        m = (sgg_ref[rr, :].astype(F32) * _dot(gl, wbg_ref[...])
             + sgs_ref[rr, :].astype(F32) * _dot(y, wbs_ref[...]))
        out_ref[rr, :] = x_ref[rr, :] + _dot(m.astype(BF16), wo_ref[...])

    tri = _tri_mask(C)
    lmat = jnp.where(tri, 1.0, 0.0).astype(BF16)
    head_of_lane = _lane_block(GLA_QK, GLA_DK)
    group_of_lane = _lane_block(SSM_BC, SSM_STATE)
    lane128 = lax.broadcasted_iota(jnp.int32, (1, SMALL_W), 1)
    fwd_lane = lane128 < LANE_DTB
    lane_lo = lane128 < SSM_HEAD_DIM
    p2 = p2_ref[...]
    zero_bf = jnp.zeros((), BF16)

    def chunk_a(c):
        rows = slice(c * C, (c + 1) * C)
        la = la_ref[rows, :]
        lg = lg_ref[rows, :]
        cs = _dot_exact_lhs(lmat, jnp.concatenate([lg, la], axis=1))
        ics, ics_b, af = cs[:, :GLA_QK], cs[:, GLA_QK:2 * GLA_QK], cs[:, 2 * GLA_QK:]
        rcs = ics_b[C - 1:C, :] - ics_b + lg[:, GLA_QK:]
        rb = af[C - 1:C, :] - af + la
        af_s[rows, :] = af
        rb_s[rows, :] = rb

        m_f = ics[C // 2 - 1:C // 2, :]
        tot_f = ics[C - 1:C, :]
        m_b = rcs[C // 2:C // 2 + 1, :]
        tot_b = rcs[0:1, :]
        q = qk_ref[rows, 0:GLA_QK].astype(F32)
        k = qk_ref[rows, GLA_QK:2 * GLA_QK].astype(F32)
        qf32 = q * jnp.exp2(ics - m_f)
        kf32 = k * jnp.exp2(m_f - ics)
        qb32 = q * jnp.exp2(rcs - m_b)
        kb32 = k * jnp.exp2(m_b - rcs)
        kfs = (kf32 * jnp.exp2(tot_f - m_f)).astype(BF16)
        kbs = (kb32 * jnp.exp2(tot_b - m_b)).astype(BF16)
        qf_s[rows, :] = qf32.astype(BF16)
        qb_s[rows, :] = qb32.astype(BF16)
        kf_s[rows, :] = kf32.astype(BF16)
        kb_s[rows, :] = kb32.astype(BF16)
        qfi_s[rows, :] = (qf32 * jnp.exp2(m_f)).astype(BF16)
        qbi_s[rows, :] = (qb32 * jnp.exp2(m_b)).astype(BF16)
        etf_s[c] = jnp.broadcast_to(jnp.exp2(tot_f), (8, GLA_QK))
        etb_s[c] = jnp.broadcast_to(jnp.exp2(tot_b), (8, GLA_QK))
        vt_kf = _dot_tn(v_ref[rows, :], kfs)
        vt_kb = _dot_tn(v_ref[rows, :], kbs)
        kvf = jnp.zeros((GLA_DV, GLA_QK), F32)
        kvb = jnp.zeros((GLA_DV, GLA_QK), F32)
        for h in range(GLA_HEADS):
            hm = head_of_lane == h
            kvf = kvf + jnp.where(hm, vt_kf[h * GLA_DV:(h + 1) * GLA_DV, :], 0.0)
            kvb = kvb + jnp.where(hm, vt_kb[h * GLA_DV:(h + 1) * GLA_DV, :], 0.0)
        kvf_s[c] = kvf
        kvb_s[c] = kvb

        tot_a = af[C - 1:C, :]
        tot_r = rb[0:1, :]
        sw = jnp.exp2(jnp.where(fwd_lane, tot_a - af, tot_r - rb) + ldt_ref[rows, :])
        return sw.astype(BF16), jnp.exp2(jnp.where(fwd_lane, tot_a, tot_r))

    def chunk_a2(c, swx):
        rows = slice(c * C, (c + 1) * C)
        xsf = xa_ref[rows, 0:SSM_INNER].astype(F32)
        bm = xa_ref[rows, SSM_INNER:SSM_INNER + SSM_BC]
        xwf = (xsf * swx[:, :SSM_INNER].astype(F32)).astype(BF16)
        xwb = (xsf * swx[:, SSM_INNER:].astype(F32)).astype(BF16)
        bms = [bm[:, g * SSM_STATE:(g + 1) * SSM_STATE] for g in range(SSM_GROUPS)]
        dsf_s[c] = jnp.concatenate(
            [_dot_tn(bms[g], xwf[:, g * 256:(g + 1) * 256]) for g in range(SSM_GROUPS)], axis=0)
        dsb_s[c] = jnp.concatenate(
            [_dot_tn(bms[g], xwb[:, g * 256:(g + 1) * 256]) for g in range(SSM_GROUPS)], axis=0)

    n_fin = 2 if nc % 2 == 0 else 1
    fin_in = [finish_norms(p_, n_fin) for p_ in range(n_fin)]
    sws, ets = [], []
    for c in range(nc):
        sw_c, et_c = chunk_a(c)
        sws.append(sw_c)
        ets.append(et_c)
        if n_fin == 2 and c == nc // 2 - 1:
            finish_merge(0, 2, *fin_in[0])
    et_hi, et_lo = _split2(jnp.concatenate(ets + [jnp.zeros((8 - nc, SMALL_W), F32)], axis=0))
    exp_all = _dot(jnp.concatenate(sws + [et_hi, et_lo], axis=0), p2)
    swx_all = exp_all[0:t, :].astype(BF16)
    etx = exp_all[t:t + 8, :] + exp_all[t + 8:t + 16, :]
    for c in range(nc):
        chunk_a2(c, swx_all[c * C:(c + 1) * C, :])
        decf_s[c] = _stack_row(etx[c:c + 1, :SSM_INNER])
        decb_s[c] = _stack_row(etx[c:c + 1, SSM_INNER:])
        if c == nc // 2 - 1 or nc == 1:
            finish_merge(n_fin - 1, n_fin, *fin_in[n_fin - 1])

    sg = sgf_s[...]
    ss = ssf_s[...]
    for c in range(nc):
        stf_s[c] = sg.astype(BF16)
        ssf_c[c] = ss.astype(BF16)
        sg = sg * etf_s[c][0:1, :] + kvf_s[c]
        ss = ss * decf_s[c] + dsf_s[c]
    sgf_s[...] = sg
    ssf_s[...] = ss
    sg = sgb_ref[...]
    ss = ssb_ref[...]
    for c in reversed(range(nc)):
        stb_s[c] = sg.astype(BF16)
        ssb_c[c] = ss.astype(BF16)
        sg = sg * etb_s[c][0:1, :] + kvb_s[c]
        ss = ss * decb_s[c] + dsb_s[c]

    tri4 = jnp.concatenate([tri] * 4, axis=1)

    def chunk_o(c):
        rows = slice(c * C, (c + 1) * C)
        kf = kf_s[rows, :]
        kb = kb_s[rows, :]
        kf_m = jnp.concatenate([jnp.where(head_of_lane == h, kf, zero_bf) for h in range(GLA_HEADS)], axis=0)
        kb_m = jnp.concatenate([jnp.where(head_of_lane == h, kb, zero_bf) for h in range(GLA_HEADS)], axis=0)
        att = jnp.where(tri4, _dot_nt(qf_s[rows, :], kf_m), _dot_nt(qb_s[rows, :], kb_m)).astype(BF16)
        stf = stf_s[c]
        stb = stb_s[c]

        def gla_head(h):
            blk = slice((h // 2) * LANES, (h // 2 + 1) * LANES)
            hm = (lane128 // GLA_DK) == (h % 2)
            qi = jnp.concatenate([jnp.where(hm, qfi_s[rows, blk], zero_bf),
                                  jnp.where(hm, qbi_s[rows, blk], zero_bf)], axis=1)
            st = jnp.concatenate([stf[:, blk], stb[:, blk]], axis=1)
            vh = v_ref[rows, h * GLA_DV:(h + 1) * GLA_DV]
            o_s[slot_w, rows, h * GLA_DV:(h + 1) * GLA_DV] = _dot(att[:, h * C:(h + 1) * C], vh) + _dot_nt(qi, st)

        xs = xa_ref[rows, 0:SSM_INNER]
        bm = xa_ref[rows, SSM_INNER:SSM_INNER + SSM_BC]
        cm = xa_ref[rows, SSM_INNER + SSM_BC:SSM_XBC]
        af = af_s[rows, :]
        rb = rb_s[rows, :]
        ldt_t = ldt_ref[rows, :].T
        arow_t = af.T - ldt_t
        rrow_t = rb.T - ldt_t
        bm_m = jnp.concatenate([jnp.where(group_of_lane == g, bm, zero_bf) for g in range(SSM_GROUPS)], axis=0)
        cbm_all = _dot_nt(cm, bm_m)
        cm32 = cm.astype(F32)
        ssf = ssf_c[c]
        ssb = ssb_c[c]
        for g in range(SSM_GROUPS):
            gla_head(g)
            cbm = cbm_all[:, g * C:(g + 1) * C]
            cblk = cm32[:, (g // 2) * LANES:(g // 2 + 1) * LANES]
            crot = pltpu.roll(cblk, SSM_STATE, 1)
            cm2 = jnp.where(lane_lo, cblk, crot) if g % 2 == 0 else jnp.where(lane_lo, crot, cblk)
            srow = slice(g * SSM_STATE, (g + 1) * SSM_STATE)
            for pair in range(SSM_HPG // 2):
                blk = (g * SSM_HPG + pair * 2) * SSM_HEAD_DIM
                pcol = slice(pair * LANES, (pair + 1) * LANES)
                rhs = jnp.concatenate([xs[:, blk:blk + LANES], ssf[srow, pcol], ssb[srow, pcol]], axis=0)
                lhs = []
                for jj in range(2):
                    hd = g * SSM_HPG + pair * 2 + jj
                    lf = LANE_DTF + hd
                    lb = LANE_DTB + hd
                    a_col = jnp.broadcast_to(af[:, lf:lf + 1], (C, C))
                    r_col = jnp.broadcast_to(rb[:, lb:lb + 1], (C, C))
                    diff = jnp.where(tri, a_col - arow_t[lf:lf + 1, :], r_col - rrow_t[lb:lb + 1, :])
                    mm = (cbm * jnp.exp2(diff)).astype(BF16)
                    ecol = jnp.exp2(jnp.where(lane_lo, a_col, r_col))
                    lhs.append(jnp.concatenate([mm, (cm2 * ecol).astype(BF16)], axis=1))
                yy = _dot(jnp.concatenate(lhs, axis=0), rhs)
                xpair = xs[:, blk:blk + LANES].astype(F32)
                o_s[slot_w, rows, GLA_V + blk:GLA_V + blk + LANES] = (
                    jnp.where(lane_lo, yy[0:C, :], yy[C:2 * C, :]) + dskip_ref[:, blk:blk + LANES] * xpair)

    for c in range(nc):
        chunk_o(c)


def _mixer(x, qk, v, sog, sz, xa, sgg, sgs, lg, la, ldt, sgb, ssb, consts, seq_len, t):
    n = x.shape[0]
    ns = seq_len // t
    nsteps = n // t
    nc = t // CHUNK
    cur = lambda g: (jnp.minimum(g, nsteps - 1), 0)
    lag = lambda g: (jnp.maximum(g - 1, 0), 0)
    tok = lambda w: pl.BlockSpec((t, w), cur)
    tok_lag = lambda w: pl.BlockSpec((t, w), lag)
    st = lambda r, c: pl.BlockSpec((None, r, c), lambda g: (jnp.minimum(g, nsteps - 1), 0, 0))
    scratch = [
        pltpu.VMEM((GLA_DV, GLA_QK), F32),
        pltpu.VMEM((SSM_BC, 256), F32),
        pltpu.VMEM((t, SMALL_W), F32),
        pltpu.VMEM((t, SMALL_W), F32),
        pltpu.VMEM((t, GLA_QK), BF16),
        pltpu.VMEM((t, GLA_QK), BF16),
        pltpu.VMEM((t, GLA_QK), BF16),
        pltpu.VMEM((t, GLA_QK), BF16),
        pltpu.VMEM((t, GLA_QK), BF16),
        pltpu.VMEM((t, GLA_QK), BF16),
        pltpu.VMEM((nc, 8, GLA_QK), F32),
        pltpu.VMEM((nc, 8, GLA_QK), F32),
        pltpu.VMEM((nc, GLA_DV, GLA_QK), F32),
        pltpu.VMEM((nc, GLA_DV, GLA_QK), F32),
        pltpu.VMEM((nc, GLA_DV, GLA_QK), BF16),
        pltpu.VMEM((nc, GLA_DV, GLA_QK), BF16),
        pltpu.VMEM((nc, SSM_BC, 256), F32),
        pltpu.VMEM((nc, SSM_BC, 256), F32),
        pltpu.VMEM((nc, SSM_BC, 256), F32),
        pltpu.VMEM((nc, SSM_BC, 256), F32),
        pltpu.VMEM((nc, SSM_BC, 256), BF16),
        pltpu.VMEM((nc, SSM_BC, 256), BF16),
        pltpu.VMEM((2, t, GLA_V + SSM_INNER), F32),
    ]
    return pl.pallas_call(
        functools.partial(_mixer_kernel, ns=ns),
        grid=(nsteps + 1,),
        in_specs=[tok_lag(D_MODEL), tok(2 * GLA_QK), tok(GLA_V), tok_lag(GLA_V), tok_lag(SSM_INNER), tok(SSM_XBC),
                  tok_lag(D_MODEL), tok_lag(D_MODEL), tok(2 * GLA_QK), tok(SMALL_W), tok(SMALL_W),
                  st(GLA_DV, GLA_QK), st(SSM_BC, 256)]
                 + [_const_spec(a.shape) for a in consts],
        out_specs=tok_lag(D_MODEL),
        out_shape=jax.ShapeDtypeStruct((n, D_MODEL), F32),
        scratch_shapes=scratch,
        compiler_params=pltpu.CompilerParams(dimension_semantics=("arbitrary",), vmem_limit_bytes=VMEM_LIMIT),
        name="mixer",
    )(x, qk, v, sog, sz, xa, sgg, sgs, lg, la, ldt, sgb, ssb, *consts)


FFN_COL_CHUNKS = ((0, 768), (768, 1536), (1536, 2304), (2304, D_FF))


def _ffn_kernel(x_ref, nw_ref, wg_ref, wu_ref, wd_ref, fw_ref, out_ref):
    x = x_ref[...]
    h = _rms(x, nw_ref[...]).astype(BF16)
    y = x
    for lo, hi in FFN_COL_CHUNKS:
        a = (_silu(_dot(h, wg_ref[:, lo:hi])) * _dot(h, wu_ref[:, lo:hi])).astype(BF16)
        y = y + _dot(a, wd_ref[lo:hi, :])
    out_ref[...] = _rms(y, fw_ref[...])


def _ffn(x2d, nw, wg, wu, wd, fw, tm):
    n = x2d.shape[0]
    row = lambda i: (i, 0)
    return pl.pallas_call(
        _ffn_kernel,
        grid=(n // tm,),
        in_specs=[pl.BlockSpec((tm, D_MODEL), row)] + [_const_spec(a.shape) for a in (nw, wg, wu, wd, fw)],
        out_specs=pl.BlockSpec((tm, D_MODEL), row),
        out_shape=jax.ShapeDtypeStruct((n, D_MODEL), F32),
        compiler_params=pltpu.CompilerParams(dimension_semantics=("arbitrary",), vmem_limit_bytes=VMEM_LIMIT),
        name="ffn",
    )(x2d, nw, wg, wu, wd, fw)


def _tile(n, pref):
    t = pref
    while n % t:
        t //= 2
    return t


def _prep_weights(norm_mix_w, w_in, gla_up_f, gla_bias_f, gla_up_b, gla_bias_b, gla_norm_w, conv_w, conv_b,
                  dt_bias_f, dt_bias_b, a_log_f, a_log_b, d_skip, ssm_norm_w, w_br_gla, w_br_ssm, w_out,
                  norm_ffn_w, w_ffn_gate, w_ffn_up, w_ffn_down, norm_final_w):
    offs = [0]
    for sz in IN_SIZES:
        offs.append(offs[-1] + sz)
    w_in = w_in.astype(BF16)
    col = lambda i: w_in[:, offs[i]:offs[i + 1]]
    wq, wk, wv, wog, wrf, wrb, wz, wxbc, wdtf, wdtb, wgg, wgs = [col(i) for i in range(12)]
    wqk = jnp.concatenate([wq * (GLA_DK ** -0.5), wk], axis=1)
    wsm = jnp.concatenate([wrf, wrb, wdtf, wdtb, jnp.zeros((D_MODEL, SMALL_W - 64), BF16)], axis=1)
    proj_ws = [w.astype(BF16) for w in (wqk, wv, 0.5 * wog, 0.5 * wz, wxbc, 0.5 * wgg, 0.5 * wgs, wsm)]

    upc = jnp.zeros((SMALL_W, 2 * GLA_QK), F32)
    upc = upc.at[LANE_RF:LANE_RF + GLA_GATE_RANK, :GLA_QK].set(gla_up_f)
    upc = upc.at[LANE_RB:LANE_RB + GLA_GATE_RANK, GLA_QK:].set(gla_up_b)
    upc = upc.astype(BF16)
    gbias = jnp.concatenate([gla_bias_f, gla_bias_b])[None, :]

    def small_row(f, b):
        r = jnp.zeros((1, SMALL_W), F32)
        r = r.at[0, LANE_DTF:LANE_DTF + SSM_HEADS].set(f)
        return r.at[0, LANE_DTB:LANE_DTB + SSM_HEADS].set(b)

    dtbias = small_row(dt_bias_f, dt_bias_b)
    alog = small_row(a_log_f, a_log_b)

    lane = jnp.arange(SMALL_W)[:, None]
    blk = jnp.arange(SSM_INNER)[None, :] // SSM_HEAD_DIM
    p2f = (lane == LANE_DTF + blk)
    p2b = (lane == LANE_DTB + blk)
    p2 = jnp.concatenate([p2f, p2b], axis=1).astype(BF16)
    p2b = p2b.astype(BF16)

    cwh = 0.5 * jnp.concatenate([conv_w, jnp.zeros((8 - SSM_CONV, SSM_XBC), F32)], axis=0)
    return dict(
        nw=norm_mix_w[None, :], proj_ws=proj_ws,
        proj_consts=(upc, gbias, dtbias, alog, cwh, 0.5 * conv_b[None, :]),
        p2=p2, p2b=p2b,
        glanw=jnp.tile(gla_norm_w, GLA_HEADS)[None, :], dskip=jnp.repeat(d_skip, SSM_HEAD_DIM)[None, :],
        ssmnw=ssm_norm_w[None, :], wbg=w_br_gla.astype(BF16), wbs=w_br_ssm.astype(BF16),
        wo=w_out.astype(BF16), fnw=norm_ffn_w[None, :], wg=w_ffn_gate.astype(BF16),
        wu=w_ffn_up.astype(BF16), wd=w_ffn_down.astype(BF16), finw=norm_final_w[None, :])


def _run(x, p):
    b, l, _ = x.shape
    n = b * l
    t = _tile(l, 512)
    x2d = x.reshape(n, D_MODEL)
    qk, v, sog, sz, xa, sgg, sgs, lg, la, ldt = _inproj(x2d, l, p["nw"], p["proj_ws"], p["proj_consts"], t)
    sgb, ssb = _bwd_states(qk, v, xa, lg, la, ldt, p["p2b"], l, t)
    consts = (p["glanw"], p["dskip"], p["ssmnw"], p["p2"], p["wbg"], p["wbs"], p["wo"])
    xm = _mixer(x2d, qk, v, sog, sz, xa, sgg, sgs, lg, la, ldt, sgb, ssb, consts, l, t)
    y = _ffn(xm, p["fnw"], p["wg"], p["wu"], p["wd"], p["finw"], _tile(n, 1024))
    return y.reshape(b, l, D_MODEL)


def kernel(x_prompt, x_sample, norm_mix_w, w_in, gla_up_f, gla_bias_f, gla_up_b, gla_bias_b, gla_norm_w, conv_w,
           conv_b, dt_bias_f, dt_bias_b, a_log_f, a_log_b, d_skip, ssm_norm_w, w_br_gla, w_br_ssm, w_out,
           norm_ffn_w, w_ffn_gate, w_ffn_up, w_ffn_down, norm_final_w):
    assert norm_mix_w.shape[0] == 1, "single-layer block"
    p = _prep_weights(norm_mix_w[0], w_in.reshape(w_in.shape[1:]), gla_up_f[0], gla_bias_f[0], gla_up_b[0], gla_bias_b[0],
                      gla_norm_w[0], conv_w[0], conv_b[0], dt_bias_f[0], dt_bias_b[0], a_log_f[0], a_log_b[0],
                      d_skip[0], ssm_norm_w[0], w_br_gla[0], w_br_ssm[0], w_out[0], norm_ffn_w[0],
                      w_ffn_gate[0], w_ffn_up[0], w_ffn_down[0], norm_final_w)
    return (_run(x_prompt, p), _run(x_sample, p))
```
